```python
import math
import jax
import jax.numpy as jnp
from jax import lax
import numpy as np


D_MODEL = 1024
BATCH = 2
SEQ = 8192
DEPTH = 1

HEAD_DIM = 64
HEADS_PER_GROUP = 4
DILATION_GROUPS = ((128, 1), (512, 4), (2048, 16))
N_GROUPS = len(DILATION_GROUPS)
ATTN_WIDTH = N_GROUPS * HEADS_PER_GROUP * HEAD_DIM
ATTN_OUT = HEADS_PER_GROUP * HEAD_DIM
ROPE_THETA = 500000.0
ROPE_DIM = HEAD_DIM // 4
NEG_BIG = -1e30

HYENA_WIDTH = D_MODEL // 2
HYENA_ORDER = 2
HYENA_EMB = 33
HYENA_BANDS = (HYENA_EMB - 1) // 2
HYENA_FILTER_HIDDEN = 64
HYENA_SHORT = 3
HYENA_DECAY_SHORT_PCT = 0.3
HYENA_DECAY_LONG_PCT = 1.5
HYENA_DECAY_TARGET = 1e-2
HYENA_FILTER_INIT = 0.05

PEER_HEADS = 8
PEER_NKEYS = 128
PEER_EXPERTS = PEER_NKEYS * PEER_NKEYS
PEER_KEY_DIM = 128
PEER_TOPK = 16
PEER_CHUNK = 128

N_BRANCH = 2
IN_WIDTH = (HYENA_ORDER + 1) * HYENA_WIDTH + 3 * ATTN_WIDTH + N_BRANCH * D_MODEL
DN_ALPHA = (2.0 * DEPTH) ** 0.25
DN_BETA = (8.0 * DEPTH) ** -0.25
ADA_INIT = 0.2
LN_EPS = 1e-5

kernel_name = 'hybrid_hyena_dilated_attn_peer_encoder'


def layer_norm(x, g, b):
    xf = x.astype(jnp.float32)
    mu = jnp.mean(xf, axis=-1, keepdims=True)
    var = jnp.mean(jnp.square(xf - mu), axis=-1, keepdims=True)
    y = (xf - mu) * lax.rsqrt(var + LN_EPS) * g.astype(jnp.float32) + b.astype(jnp.float32)
    return y.astype(x.dtype)


def short_conv(z, w, b):
    zp = jnp.pad(z, ((0, 0), (1, 1), (0, 0)))
    return zp[:, :-2] * w[0] + zp[:, 1:-1] * w[1] + zp[:, 2:] * w[2] + b


def hyena_filters(L, f1_w, f1_b, f1_freq, f2_w, f2_b, f2_freq, f3_w):
    f32 = jnp.float32
    t = jnp.linspace(0.0, 1.0, L, dtype=f32)[:, None]
    w = 2.0 * math.pi * jnp.arange(L, dtype=f32)[:, None] / L
    f = jnp.linspace(1e-4, HYENA_BANDS - 1, HYENA_BANDS, dtype=f32)[None, :]
    z = jnp.concatenate([t, jnp.cos(f * w), -jnp.sin(f * w)], axis=-1)
    h = jnp.sin(f1_freq.astype(f32) * (z @ f1_w.astype(f32) + f1_b.astype(f32)))
    h = jnp.sin(f2_freq.astype(f32) * (h @ f2_w.astype(f32) + f2_b.astype(f32)))
    h = (h @ f3_w.astype(f32)).reshape(L, 2, HYENA_ORDER, HYENA_WIDTH)
    max_decay = math.log(HYENA_DECAY_TARGET) / HYENA_DECAY_SHORT_PCT
    min_decay = math.log(HYENA_DECAY_TARGET) / HYENA_DECAY_LONG_PCT
    deltas = jnp.linspace(min_decay, max_decay, HYENA_WIDTH, dtype=f32)
    decay = jnp.exp(-t * jnp.abs(deltas)[None, :])
    h = h * decay[:, None, None, :]
    fwd, bwd = h[:, 0], h[:, 1]
    k2 = jnp.concatenate([fwd, jnp.zeros((1, HYENA_ORDER, HYENA_WIDTH), f32), bwd[:0:-1]], axis=0)
    return jnp.fft.rfft(k2, axis=0)


def long_conv(z, kf, bias):
    L = z.shape[1]
    zf = jnp.fft.rfft(z.astype(jnp.float32), n=2 * L, axis=1)
    y = jnp.fft.irfft(zf * kf[None], n=2 * L, axis=1)[:, :L]
    return (y + z.astype(jnp.float32) * bias.astype(jnp.float32)).astype(z.dtype)


def hyena_mixer(p_h, conv_w, conv_b, f1_w, f1_b, f1_freq, f2_w, f2_b, f2_freq, f3_w, bias):
    zc = short_conv(p_h, conv_w, conv_b)
    v, x1, x2 = jnp.split(zc, 3, axis=-1)
    K = hyena_filters(p_h.shape[1], f1_w, f1_b, f1_freq, f2_w, f2_b, f2_freq, f3_w)
    z = v
    for o, gate in enumerate((x1, x2)):
        z = gate * long_conv(z, K[:, o], bias[o])
    return z


def rotary(t, positions):
    f32 = jnp.float32
    inv = ROPE_THETA ** (-jnp.arange(0, ROPE_DIM, 2, dtype=f32) / ROPE_DIM)
    ang = positions.astype(f32)[..., None] * inv
    cos = jnp.cos(ang)[:, :, None, :]
    sin = jnp.sin(ang)[:, :, None, :]
    half = ROPE_DIM // 2
    t1 = t[..., :half].astype(f32)
    t2 = t[..., half:ROPE_DIM].astype(f32)
    out = jnp.concatenate([t1 * cos - t2 * sin, t2 * cos + t1 * sin, t[..., ROPE_DIM:].astype(f32)], axis=-1)
    return out.astype(t.dtype)


def dilated_window_attention(q, k, v, dilation, half):
    B, S, H, E = q.shape
    n = S // dilation
    blk = half
    nb = -(-n // blk)
    n_pad = nb * blk
    f32 = jnp.float32

    def to_blocks(t):
        t = t.reshape(B, n, dilation, H, E).transpose(0, 2, 1, 3, 4)
        t = jnp.pad(t, ((0, 0), (0, 0), (0, n_pad - n), (0, 0), (0, 0)))
        return t.reshape(B, dilation, nb, blk, H, E)

    def windows(t):
        tp = jnp.pad(t, ((0, 0), (0, 0), (1, 1), (0, 0), (0, 0), (0, 0)))
        return jnp.concatenate([tp[:, :, :-2], tp[:, :, 1:-1], tp[:, :, 2:]], axis=3)

    def from_blocks(t):
        t = t.reshape((B, dilation, n_pad) + t.shape[4:])[:, :, :n]
        t = jnp.swapaxes(t, 1, 2)
        return t.reshape((B, S) + t.shape[3:])

    qb = to_blocks(q).astype(f32) * (E ** -0.5)
    kw = windows(to_blocks(k)).astype(f32)
    vw = windows(to_blocks(v)).astype(f32)
    s = jnp.einsum('brnqhe,brnkhe->brnhqk', qb, kw)
    qi = jnp.arange(nb)[:, None] * blk + jnp.arange(blk)[None, :]
    kj = (jnp.arange(nb)[:, None] - 1) * blk + jnp.arange(3 * blk)[None, :]
    mask = (jnp.abs(qi[:, :, None] - kj[:, None, :]) <= half) & (kj[:, None, :] >= 0) & (kj[:, None, :] < n)
    mask = mask[None, None, :, None]
    s = jnp.where(mask, s, NEG_BIG)
    m = jnp.max(s, axis=-1)
    p = jnp.where(mask, jnp.exp(s - m[..., None]), 0.0)
    l = jnp.sum(p, axis=-1)
    acc = jnp.einsum('brnhqk,brnkhe->brnqhe', p, vw)
    acc = from_blocks(acc)
    m = from_blocks(jnp.swapaxes(m, 3, 4))
    l = from_blocks(jnp.swapaxes(l, 3, 4))
    return acc, m, l


def attention_mixer(qkv, positions):
    B, S, _ = qkv.shape
    qkv = qkv.reshape(B, S, 3, N_GROUPS * HEADS_PER_GROUP, HEAD_DIM)
    q = rotary(qkv[:, :, 0], positions)
    k = rotary(qkv[:, :, 1], positions)
    v = qkv[:, :, 2]
    accs, ms, ls = [], [], []
    for g, (window, dilation) in enumerate(DILATION_GROUPS):
        hs = slice(g * HEADS_PER_GROUP, (g + 1) * HEADS_PER_GROUP)
        a, m, l = dilated_window_attention(q[:, :, hs], k[:, :, hs], v[:, :, hs], dilation, window // (2 * dilation))
        accs.append(a)
        ms.append(m)
        ls.append(l)
    ms = jnp.stack(ms)
    wts = jnp.exp(ms - jnp.max(ms, axis=0, keepdims=True))
    num = jnp.sum(wts[..., None] * jnp.stack(accs), axis=0)
    den = jnp.sum(wts * jnp.stack(ls), axis=0)
    out = num / den[..., None]
    return out.reshape(B, S, ATTN_OUT).astype(qkv.dtype)


def peer(u, wq, keys, utab, vtab):
    B, S, D = u.shape
    T = B * S
    xt = u.reshape(T, D)
    q = (xt @ wq).reshape(T, PEER_HEADS, 2, PEER_KEY_DIM)
    s = jnp.einsum('thpe,hpke->thpk', q.astype(jnp.float32), keys.astype(jnp.float32))
    sv, si = lax.top_k(s, PEER_TOPK)
    cand = (sv[:, :, 0, :, None] + sv[:, :, 1, None, :]).reshape(T, PEER_HEADS, PEER_TOPK * PEER_TOPK)
    cidx = (si[:, :, 0, :, None] * PEER_NKEYS + si[:, :, 1, None, :]).reshape(T, PEER_HEADS, PEER_TOPK * PEER_TOPK)
    tv, ti = lax.top_k(cand, PEER_TOPK)
    eidx = jnp.take_along_axis(cidx, ti, axis=-1)
    g = jax.nn.softmax(tv, axis=-1)
    nc = T // PEER_CHUNK
    eidx = eidx.reshape(nc, PEER_CHUNK, PEER_HEADS * PEER_TOPK)
    g = g.reshape(nc, PEER_CHUNK, PEER_HEADS * PEER_TOPK)
    xc = xt.reshape(nc, PEER_CHUNK, D)

    def chunk(args):
        xb, ib, gb = args
        a = jnp.einsum('cd,ced->ce', xb, utab[ib]).astype(jnp.float32)
        coef = (gb * jax.nn.gelu(a, approximate=False)).astype(xb.dtype)
        return jnp.einsum('ce,ced->cd', coef, vtab[ib])

    return lax.map(chunk, (xc, eidx, g)).reshape(B, S, D)


def setup_inputs(seed: int = 0) -> dict:
    key = jax.random.key(seed)
    ks = jax.random.split(key, 32)
    f32 = jnp.float32

    def nrm(k, shape, scale):
        return jax.random.normal(k, shape, f32) * scale

    L, D, HW, HFH = DEPTH, D_MODEL, HYENA_WIDTH, HYENA_FILTER_HIDDEN
    return {
        'x': nrm(ks[0], (BATCH, SEQ, D), 1.0),
        'c': nrm(ks[1], (BATCH, D), 1.0),
        'positions': jnp.broadcast_to(jnp.arange(SEQ, dtype=jnp.int32)[None, :], (BATCH, SEQ)),
        'w_ada': nrm(ks[2], (L, D, 6 * D), ADA_INIT * D ** -0.5),
        'b_ada': nrm(ks[3], (L, 6 * D), 0.02),
        'w_in': nrm(ks[4], (L, D, IN_WIDTH), D ** -0.5),
        'hy_conv_w': nrm(ks[5], (L, HYENA_SHORT, 3 * HW), HYENA_SHORT ** -0.5),
        'hy_conv_b': nrm(ks[6], (L, 3 * HW), 0.02),
        'hy_f1_w': nrm(ks[7], (L, HYENA_EMB, HFH), HYENA_EMB ** -0.5),
        'hy_f1_b': nrm(ks[8], (L, HFH), 0.1),
        'hy_f1_freq': 1.0 + nrm(ks[9], (L, HFH), 0.01),
        'hy_f2_w': nrm(ks[10], (L, HFH, HFH), HFH ** -0.5),
        'hy_f2_b': nrm(ks[11], (L, HFH), 0.1),
        'hy_f2_freq': 1.0 + nrm(ks[12], (L, HFH), 0.01),
        'hy_f3_w': nrm(ks[13], (L, HFH, 2 * HYENA_ORDER * HW), HYENA_FILTER_INIT * HFH ** -0.5),
        'hy_bias': nrm(ks[14], (L, HYENA_ORDER, HW), 1.0),
        'w_branch_hyena': nrm(ks[15], (L, HW, D), HW ** -0.5),
        'w_branch_attn': nrm(ks[16], (L, ATTN_OUT, D), ATTN_OUT ** -0.5),
        'w_out': nrm(ks[17], (L, D, D), DN_BETA * D ** -0.5),
        'ln1_g': 1.0 + nrm(ks[18], (L, D), 0.01),
        'ln1_b': nrm(ks[19], (L, D), 0.01),
        'peer_wq': nrm(ks[20], (L, D, PEER_HEADS * 2 * PEER_KEY_DIM), D ** -0.5),
        'peer_keys': nrm(ks[21], (L, PEER_HEADS, 2, PEER_NKEYS, PEER_KEY_DIM), PEER_KEY_DIM ** -0.5),
        'peer_u': nrm(ks[22], (L, PEER_EXPERTS, D), D ** -0.5),
        'peer_v': nrm(ks[23], (L, PEER_EXPERTS, D), DN_BETA),
        'ln2_g': 1.0 + nrm(ks[24], (L, D), 0.01),
        'ln2_b': nrm(ks[25], (L, D), 0.01),
    }


def reference(x, c, positions, w_ada, b_ada, w_in, hy_conv_w, hy_conv_b, hy_f1_w, hy_f1_b, hy_f1_freq,
              hy_f2_w, hy_f2_b, hy_f2_freq, hy_f3_w, hy_bias, w_branch_hyena, w_branch_attn, w_out,
              ln1_g, ln1_b, peer_wq, peer_keys, peer_u, peer_v, ln2_g, ln2_b):
    hy_cols = (HYENA_ORDER + 1) * HYENA_WIDTH
    attn_cols = 3 * ATTN_WIDTH
    for l in range(DEPTH):
        ada = jax.nn.silu(c) @ w_ada[l] + b_ada[l]
        sh1, sc1, g1, sh2, sc2, g2 = jnp.split(ada[:, None, :], 6, axis=-1)

        u = x * (1.0 + sc1) + sh1
        proj = u @ w_in[l]
        p_h, p_qkv, p_gate = jnp.split(proj, [hy_cols, hy_cols + attn_cols], axis=-1)
        y_h = hyena_mixer(p_h, hy_conv_w[l], hy_conv_b[l], hy_f1_w[l], hy_f1_b[l], hy_f1_freq[l],
                          hy_f2_w[l], hy_f2_b[l], hy_f2_freq[l], hy_f3_w[l], hy_bias[l])
        y_a = attention_mixer(p_qkv, positions)
        gate_h, gate_a = jnp.split(jax.nn.sigmoid(p_gate), 2, axis=-1)
        merged = gate_h * (y_h @ w_branch_hyena[l]) + gate_a * (y_a @ w_branch_attn[l])
        x = layer_norm(DN_ALPHA * x + (1.0 + g1) * (merged @ w_out[l]), ln1_g[l], ln1_b[l])

        u2 = x * (1.0 + sc2) + sh2
        f = peer(u2, peer_wq[l], peer_keys[l], peer_u[l], peer_v[l])
        x = layer_norm(DN_ALPHA * x + (1.0 + g2) * f, ln2_g[l], ln2_b[l])
    return x
```

```python
import functools
import math

import numpy as np
import jax
import jax.numpy as jnp
from jax import lax
from jax.experimental import pallas as pl
from jax.experimental.pallas import tpu as pltpu

F32 = jnp.float32
BF16 = jnp.bfloat16

D_MODEL = 1024
HEAD_DIM = 64
HEADS_PER_GROUP = 4
DILATIONS = (1, 4, 16)
ATTN_HALF = 64
N_GROUPS = 3
ATTN_WIDTH = N_GROUPS * HEADS_PER_GROUP * HEAD_DIM
ATTN_OUT = HEADS_PER_GROUP * HEAD_DIM
ROPE_THETA = 500000.0
ROPE_DIM = HEAD_DIM // 4
NEG_BIG = -1e30
HYENA_WIDTH = D_MODEL // 2
HYENA_BANDS = 16
HYENA_HIDDEN = 64
DECAY_SHORT_PCT = 0.3
DECAY_LONG_PCT = 1.5
DECAY_TARGET = 1e-2
PEER_HEADS = 8
PEER_NKEYS = 128
PEER_KEY_DIM = 128
PEER_TOPK = 16
DN_ALPHA = 2.0 ** 0.25
LN_EPS = 1e-5

LANES = 128
VMEM_LIMIT = 56 * 1024 * 1024

HIGHEST = lax.Precision.HIGHEST


def _cparams(*sem):
    return pltpu.CompilerParams(dimension_semantics=sem, vmem_limit_bytes=VMEM_LIMIT)


def _ada_kernel(c_ref, w_ref, b_ref, o_ref):
    c = c_ref[...]
    s = c * jax.nn.sigmoid(c)
    o_ref[...] = jnp.dot(s, w_ref[...], precision=HIGHEST, preferred_element_type=F32) + b_ref[...]


def _ada(c, w_ada, b_ada):
    B, D = c.shape
    N = w_ada.shape[1]
    tn = 1024
    return pl.pallas_call(
        _ada_kernel,
        grid=(N // tn,),
        in_specs=[pl.BlockSpec((B, D), lambda j: (0, 0)),
                  pl.BlockSpec((D, tn), lambda j: (0, j)),
                  pl.BlockSpec((1, tn), lambda j: (0, j))],
        out_specs=pl.BlockSpec((B, tn), lambda j: (0, j)),
        out_shape=jax.ShapeDtypeStruct((B, N), F32),
        compiler_params=_cparams("parallel"),
        name="ada",
    )(c, w_ada, b_ada.reshape(1, N))


def _inproj_kernel(x_ref, sc_ref, sh_ref, pos_ref, inv_ref, whT_ref, wqkv_ref, wg_ref,
                   phT_ref, qkv_ref, gate_ref):
    u = x_ref[0] * (1.0 + sc_ref[0]) + sh_ref[0]
    ub = u.astype(BF16)
    phT_ref[0] = lax.dot_general(whT_ref[...], ub, (((1,), (1,)), ((), ())),
                                 preferred_element_type=F32)
    qkv = jnp.dot(ub, wqkv_ref[...], preferred_element_type=F32)
    ang = pos_ref[0].astype(F32) * inv_ref[...]
    cos_a = jnp.cos(ang)
    sin_a = jnp.sin(ang)
    lane = lax.broadcasted_iota(jnp.int32, (1, LANES), 1) % HEAD_DIM
    half = ROPE_DIM // 2
    s_lo = jnp.where(lane < half, -sin_a, 0.0)
    s_hi = jnp.where((lane >= half) & (lane < ROPE_DIM), sin_a, 0.0)
    n_rot = 2 * ATTN_WIDTH // LANES
    n_q = ATTN_WIDTH // LANES
    for j in range(n_rot):
        t = qkv[:, j * LANES:(j + 1) * LANES]
        r = t * cos_a + pltpu.roll(t, half, 1) * s_hi + pltpu.roll(t, LANES - half, 1) * s_lo
        if j < n_q:
            r = r * (HEAD_DIM ** -0.5)
        qkv_ref[0, :, j * LANES:(j + 1) * LANES] = r.astype(BF16)
    qkv_ref[0, :, 2 * ATTN_WIDTH:] = qkv[:, 2 * ATTN_WIDTH:].astype(BF16)
    g = jnp.dot(ub, wg_ref[...], preferred_element_type=F32)
    gate_ref[0] = jax.nn.sigmoid(g).astype(BF16)


def _rope_lane_inv():
    inv = ROPE_THETA ** (-jnp.arange(0, ROPE_DIM, 2, dtype=F32) / ROPE_DIM)
    lane = np.arange(LANES) % HEAD_DIM
    idx = lane % (ROPE_DIM // 2)
    return jnp.where(jnp.asarray(lane < ROPE_DIM), inv[idx], 0.0).reshape(1, LANES).astype(F32)


def _inproj(x, sc1, sh1, positions, w_in):
    B, S, D = x.shape
    hy_cols = 3 * HYENA_WIDTH
    qkv_cols = 3 * ATTN_WIDTH
    whT = w_in[:, :hy_cols].T.astype(BF16)
    wqkv = w_in[:, hy_cols:hy_cols + qkv_cols].astype(BF16)
    wg = w_in[:, hy_cols + qkv_cols:].astype(BF16)
    n_gate = wg.shape[1]
    tS = 256
    const = lambda b, i: (0, 0)
    return pl.pallas_call(
        _inproj_kernel,
        grid=(B, S // tS),
        in_specs=[pl.BlockSpec((1, tS, D), lambda b, i: (b, i, 0)),
                  pl.BlockSpec((1, 1, D), lambda b, i: (b, 0, 0)),
                  pl.BlockSpec((1, 1, D), lambda b, i: (b, 0, 0)),
                  pl.BlockSpec((1, tS, 1), lambda b, i: (b, i, 0)),
                  pl.BlockSpec((1, LANES), const),
                  pl.BlockSpec((hy_cols, D), const),
                  pl.BlockSpec((D, qkv_cols), const),
                  pl.BlockSpec((D, n_gate), const)],
        out_specs=[pl.BlockSpec((1, hy_cols, tS), lambda b, i: (b, 0, i)),
                   pl.BlockSpec((1, tS, qkv_cols), lambda b, i: (b, i, 0)),
                   pl.BlockSpec((1, tS, n_gate), lambda b, i: (b, i, 0))],
        out_shape=[jax.ShapeDtypeStruct((B, hy_cols, S), F32),
                   jax.ShapeDtypeStruct((B, S, qkv_cols), BF16),
                   jax.ShapeDtypeStruct((B, S, n_gate), BF16)],
        compiler_params=_cparams("parallel", "parallel"),
        name="inproj",
    )(x, sc1, sh1, positions.reshape(B, S, 1), _rope_lane_inv(), whT, wqkv, wg)


ATTN_QSUB = 128
ATTN_KWIN = 256


def _attn_kernel(q_ref, k_ref, v_ref, o_ref, lse_ref, *, n, qb):
    i = pl.program_id(2)
    head_of_lane = lax.broadcasted_iota(jnp.int32, (1, ATTN_OUT), 1) // HEAD_DIM
    for j in range(qb // ATTN_QSUB):
        q0 = i * qb + j * ATTN_QSUB
        ks = pl.multiple_of(jnp.clip(q0 - ATTN_HALF, 0, n - ATTN_KWIN), ATTN_HALF)
        kw = k_ref[0, pl.ds(ks, ATTN_KWIN), :]
        vw = v_ref[0, pl.ds(ks, ATTN_KWIN), :]
        qt = q_ref[0, j * ATTN_QSUB:(j + 1) * ATTN_QSUB, :]
        qi = q0 + lax.broadcasted_iota(jnp.int32, (ATTN_QSUB, 1), 0)
        kj = ks + lax.broadcasted_iota(jnp.int32, (1, ATTN_KWIN), 1)
        valid = jnp.abs(qi - kj) <= ATTN_HALF
        acc = jnp.zeros((ATTN_QSUB, ATTN_OUT), F32)
        lse = jnp.zeros((ATTN_QSUB, ATTN_OUT), F32)
        for h in range(HEADS_PER_GROUP):
            hm = head_of_lane == h
            qm = jnp.where(hm, qt, jnp.zeros_like(qt))
            s = lax.dot_general(qm, kw, (((1,), (1,)), ((), ())), preferred_element_type=F32)
            s = jnp.where(valid, s, NEG_BIG)
            m = jnp.max(s, axis=1, keepdims=True)
            p = jnp.exp(s - m)
            l = jnp.sum(p, axis=1, keepdims=True)
            o = jnp.dot(p.astype(BF16), vw, preferred_element_type=F32)
            acc = jnp.where(hm, o * (1.0 / l), acc)
            lse = jnp.where(hm, m + jnp.log(l), lse)
        o_ref[0, j * ATTN_QSUB:(j + 1) * ATTN_QSUB, :] = acc.astype(o_ref.dtype)
        lse_ref[0, j * ATTN_QSUB:(j + 1) * ATTN_QSUB, :] = lse


def _attn_group(qkv, g):
    B, S, W = qkv.shape
    d = DILATIONS[g]
    n = S // d
    qb = min(512, n)
    nblk = W // ATTN_OUT
    per = ATTN_WIDTH // ATTN_OUT
    view = qkv.reshape(B, n, d * W)
    kern = functools.partial(_attn_kernel, n=n, qb=qb)
    o, lse = pl.pallas_call(
        kern,
        grid=(B, d, n // qb),
        in_specs=[pl.BlockSpec((1, qb, ATTN_OUT), lambda b, r, i: (b, i, nblk * r + g)),
                  pl.BlockSpec((1, n, ATTN_OUT), lambda b, r, i: (b, 0, nblk * r + per + g)),
                  pl.BlockSpec((1, n, ATTN_OUT), lambda b, r, i: (b, 0, nblk * r + 2 * per + g))],
        out_specs=[pl.BlockSpec((1, qb, ATTN_OUT), lambda b, r, i: (b, i, r)),
                   pl.BlockSpec((1, qb, ATTN_OUT), lambda b, r, i: (b, i, r))],
        out_shape=[jax.ShapeDtypeStruct((B, n, d * ATTN_OUT), BF16),
                   jax.ShapeDtypeStruct((B, n, d * ATTN_OUT), F32)],
        compiler_params=_cparams("parallel", "parallel", "arbitrary"),
        name=f"attn_d{d}",
    )(view, view, view)
    return o.reshape(B, S, ATTN_OUT), lse.reshape(B, S, ATTN_OUT)


FFT_R = 128


def _dft_consts():
    n = np.arange(FFT_R)
    ang = 2.0 * np.pi * np.outer(n, n) / FFT_R
    fr, fi = np.cos(ang), -np.sin(ang)
    tw = 2.0 * np.pi * np.outer(n, n) / (FFT_R * FFT_R)
    twr, twi = np.cos(tw), -np.sin(tw)
    inv_n = 1.0 / (FFT_R * FFT_R)
    half = FFT_R // 2
    c = dict(
        f2=np.concatenate([fr, fi], 0),
        w3=np.block([[fr, fi], [-fi, fr]]),
        w1c=np.block([[fr, -fi], [fi, fr]]),
        c2=np.concatenate([fr[:half] * inv_n, -fi[:half] * inv_n], 0),
    )
    out = {k: jnp.asarray(v, BF16) for k, v in c.items()}
    out["twr"] = jnp.asarray(twr, F32)
    out["twi"] = jnp.asarray(twi, F32)
    return out


def _filt_mlp_kernel(tt_ref, ww_ref, f_ref, w1t_ref, w1c_ref, w1s_ref, b1_ref, fr1_ref,
                     w2_ref, b2_ref, fr2_ref, o_ref):
    t = tt_ref[0]
    fw = f_ref[...] * ww_ref[0]
    pre = (w1t_ref[...] * t
           + jnp.dot(w1c_ref[...], jnp.cos(fw), precision=HIGHEST, preferred_element_type=F32)
           + jnp.dot(w1s_ref[...], -jnp.sin(fw), precision=HIGHEST, preferred_element_type=F32))
    h1 = jnp.sin(fr1_ref[...] * (pre + b1_ref[...]))
    h2 = jnp.dot(w2_ref[...], h1, precision=HIGHEST, preferred_element_type=F32) + b2_ref[...]
    o_ref[0] = jnp.sin(fr2_ref[...] * h2)


def _filt_time_kernel(h2_ref, w3_ref, tt_ref, ad_ref, o_ref):
    half = pl.program_id(2)
    h = jnp.dot(w3_ref[0, 0], h2_ref[0], precision=HIGHEST, preferred_element_type=F32)
    decay = jnp.exp(-tt_ref[0] * ad_ref[...])
    lane = lax.broadcasted_iota(jnp.int32, (1, h.shape[1]), 1)
    o_ref[0] = jnp.where((lane == 0) & (half == 1), 0.0, h * decay)


def _filt_spec_kernel(k_ref, f2_ref, twr_ref, twi_ref, w3_ref, o_ref, *, cf):
    twr = twr_ref[...]
    twi = twi_ref[...]

    def body(c, carry):
        kt = k_ref[0, c].astype(BF16)
        m = jnp.dot(f2_ref[...], kt, preferred_element_type=F32)
        ar, ai = m[:FFT_R], m[FFT_R:]
        apr = ar * twr - ai * twi
        api = ar * twi + ai * twr
        a = jnp.concatenate([apr, api], axis=1).astype(BF16)
        o_ref[0, c] = jnp.dot(a, w3_ref[...], preferred_element_type=F32)
        return carry

    lax.fori_loop(0, cf, body, 0)


def _hyena_conv_kernel(cw_ref, cb_ref, hb_ref, v_ref, x1_ref, x2_ref, ks_ref,
                       f2_ref, twr_ref, twi_ref, w3_ref, w1c_ref, c2_ref, o_ref, *, ct, nb):
    ci = pl.program_id(0)
    rows = FFT_R // 2
    twr = twr_ref[...]
    twi = twi_ref[...]
    lane = lax.broadcasted_iota(jnp.int32, (rows, LANES), 1)
    row = lax.broadcasted_iota(jnp.int32, (rows, LANES), 0)
    n_ch = cb_ref.shape[0]
    width = hb_ref.shape[0] // 2

    def short_conv(z, ch):
        r = pltpu.roll(z, 1, 1)
        r2 = pltpu.roll(r, 1, 0)
        prev = jnp.where(lane == 0, jnp.where(row == 0, 0.0, r2), r)
        f = pltpu.roll(z, LANES - 1, 1)
        f2 = pltpu.roll(f, rows - 1, 0)
        nxt = jnp.where(lane == LANES - 1, jnp.where(row == rows - 1, 0.0, f2), f)
        return prev * cw_ref[ch] + z * cw_ref[n_ch + ch] + nxt * cw_ref[2 * n_ch + ch] + cb_ref[ch]

    def long_conv(z0, z1, kr, ki):
        zc = jnp.concatenate([z0, z1], axis=1).astype(BF16)
        m = jnp.dot(f2_ref[...], zc, preferred_element_type=F32)
        ar = m[:FFT_R, :FFT_R] - m[FFT_R:, FFT_R:]
        ai = m[FFT_R:, :FFT_R] + m[:FFT_R, FFT_R:]
        apr = ar * twr - ai * twi
        api = ar * twi + ai * twr
        xs = jnp.dot(jnp.concatenate([apr, api], axis=1).astype(BF16), w3_ref[...],
                     preferred_element_type=F32)
        xr, xi = xs[:, :FFT_R], xs[:, FFT_R:]
        yr = xr * kr - xi * ki
        yi = xr * ki + xi * kr
        bs = jnp.dot(jnp.concatenate([yr, yi], axis=1).astype(BF16), w1c_ref[...],
                     preferred_element_type=F32)
        br, bi = bs[:, :FFT_R], bs[:, FFT_R:]
        bpr = br * twr + bi * twi
        bpi = bi * twr - br * twi
        m4 = jnp.dot(c2_ref[...], jnp.concatenate([bpr, bpi], axis=1).astype(BF16),
                     preferred_element_type=F32)
        y0 = m4[:rows, :FFT_R] - m4[rows:, FFT_R:]
        y1 = m4[:rows, FFT_R:] + m4[rows:, :FFT_R]
        return y0, y1

    def body(c, carry):
        ch = ci * ct + c
        z = [short_conv(v_ref[b, c], ch) for b in range(nb)]
        for o, g_ref in enumerate((x1_ref, x2_ref)):
            gch = (o + 1) * width + ch
            ksp = ks_ref[o, c]
            y = long_conv(z[0], z[1], ksp[:, :FFT_R], ksp[:, FFT_R:])
            bias = hb_ref[o * width + ch]
            z = [short_conv(g_ref[b, c], gch) * (y[b] + z[b] * bias) for b in range(nb)]
        for b in range(nb):
            o_ref[b, c] = z[b].astype(o_ref.dtype)
        return carry

    lax.fori_loop(0, ct, body, 0)


def _hyena(phT, conv_w, conv_b, f1_w, f1_b, f1_freq, f2_w, f2_b, f2_freq, f3_w, bias):
    B, C3, L = phT.shape
    assert B == 2 and 2 * L == FFT_R * FFT_R
    HW = C3 // 3
    hid = f2_w.shape[0]
    cs = _dft_consts()
    t = jnp.linspace(0.0, 1.0, L, dtype=F32)
    w = 2.0 * math.pi * jnp.arange(L, dtype=F32) / L
    rev = (L - jnp.arange(L)) % L
    tt = jnp.stack([t, t[rev]]).reshape(2, 1, L)
    ww = jnp.stack([w, w[rev]]).reshape(2, 1, L)
    bands = jnp.linspace(1e-4, HYENA_BANDS - 1, HYENA_BANDS, dtype=F32).reshape(HYENA_BANDS, 1)
    max_decay = math.log(DECAY_TARGET) / DECAY_SHORT_PCT
    min_decay = math.log(DECAY_TARGET) / DECAY_LONG_PCT
    absdelta = jnp.abs(jnp.linspace(min_decay, max_decay, HW, dtype=F32)).reshape(HW, 1)

    w1T = f1_w.T.astype(F32)
    col = lambda a: a.astype(F32).reshape(-1, 1)
    full = lambda shp: pl.BlockSpec(shp, lambda *a: (0,) * len(shp))
    h2 = pl.pallas_call(
        _filt_mlp_kernel,
        grid=(2,),
        in_specs=[pl.BlockSpec((1, 1, L), lambda h: (h, 0, 0)),
                  pl.BlockSpec((1, 1, L), lambda h: (h, 0, 0)),
                  full((HYENA_BANDS, 1)), full((hid, 1)), full((hid, HYENA_BANDS)),
                  full((hid, HYENA_BANDS)), full((hid, 1)), full((hid, 1)),
                  full((hid, hid)), full((hid, 1)), full((hid, 1))],
        out_specs=pl.BlockSpec((1, hid, L), lambda h: (h, 0, 0)),
        out_shape=jax.ShapeDtypeStruct((2, hid, L), F32),
        compiler_params=_cparams("parallel"),
        name="hyena_filter_mlp",
    )(tt, ww, bands, w1T[:, :1], w1T[:, 1:1 + HYENA_BANDS], w1T[:, 1 + HYENA_BANDS:],
      col(f1_b), col(f1_freq), f2_w.T.astype(F32), col(f2_b), col(f2_freq))

    w3T = f3_w.T.astype(F32).reshape(2, 2, HW, hid)
    ctf = 128
    kt = pl.pallas_call(
        _filt_time_kernel,
        grid=(2, HW // ctf, 2),
        in_specs=[pl.BlockSpec((1, hid, L), lambda o, c, h: (h, 0, 0)),
                  pl.BlockSpec((1, 1, ctf, hid), lambda o, c, h: (h, o, c, 0)),
                  pl.BlockSpec((1, 1, L), lambda o, c, h: (h, 0, 0)),
                  pl.BlockSpec((ctf, 1), lambda o, c, h: (c, 0))],
        out_specs=pl.BlockSpec((1, ctf, L), lambda o, c, h: (o, c, h)),
        out_shape=jax.ShapeDtypeStruct((2, HW, 2 * L), F32),
        compiler_params=_cparams("parallel", "parallel", "arbitrary"),
        name="hyena_filter_time",
    )(h2, w3T, tt, absdelta)

    cf = 16
    kspec = pl.pallas_call(
        functools.partial(_filt_spec_kernel, cf=cf),
        grid=(2, HW // cf),
        in_specs=[pl.BlockSpec((1, cf, FFT_R, FFT_R), lambda o, c: (o, c, 0, 0)),
                  full((2 * FFT_R, FFT_R)), full((FFT_R, FFT_R)), full((FFT_R, FFT_R)),
                  full((2 * FFT_R, 2 * FFT_R))],
        out_specs=pl.BlockSpec((1, cf, FFT_R, 2 * FFT_R), lambda o, c: (o, c, 0, 0)),
        out_shape=jax.ShapeDtypeStruct((2, HW, FFT_R, 2 * FFT_R), F32),
        compiler_params=_cparams("parallel", "parallel"),
        name="hyena_filter_spec",
    )(kt.reshape(2, HW, FFT_R, FFT_R), cs["f2"], cs["twr"], cs["twi"], cs["w3"])

    ct = 8
    rows = FFT_R // 2
    ph4 = phT.reshape(B, C3, rows, LANES)
    nblk = HW // ct
    smem = pl.BlockSpec(memory_space=pltpu.SMEM)
    yh = pl.pallas_call(
        functools.partial(_hyena_conv_kernel, ct=ct, nb=B),
        grid=(nblk,),
        in_specs=[smem, smem, smem,
                  pl.BlockSpec((B, ct, rows, LANES), lambda c: (0, c, 0, 0)),
                  pl.BlockSpec((B, ct, rows, LANES), lambda c: (0, c + nblk, 0, 0)),
                  pl.BlockSpec((B, ct, rows, LANES), lambda c: (0, c + 2 * nblk, 0, 0)),
                  pl.BlockSpec((2, ct, FFT_R, 2 * FFT_R), lambda c: (0, c, 0, 0)),
                  full((2 * FFT_R, rows)), full((FFT_R, FFT_R)), full((FFT_R, FFT_R)),
                  full((2 * FFT_R, 2 * FFT_R)), full((2 * FFT_R, 2 * FFT_R)), full((FFT_R, FFT_R))],
        out_specs=pl.BlockSpec((B, ct, rows, LANES), lambda c: (0, c, 0, 0)),
        out_shape=jax.ShapeDtypeStruct((B, HW, rows, LANES), BF16),
        compiler_params=_cparams("parallel"),
        name="hyena_conv",
    )(conv_w.astype(F32).reshape(-1), conv_b.astype(F32), bias.astype(F32).reshape(-1),
      ph4, ph4, ph4, kspec, cs["f2"][:, :rows], cs["twr"], cs["twi"], cs["w3"], cs["w1c"], cs["c2"])
    return yh.reshape(B, HW, L)


def _layer_norm(r, g, b):
    mu = jnp.mean(r, axis=-1, keepdims=True)
    d = r - mu
    var = jnp.mean(d * d, axis=-1, keepdims=True)
    return d * lax.rsqrt(var + LN_EPS) * g + b


def _merge_kernel(x_ref, yh_ref, o0_ref, l0_ref, o1_ref, l1_ref, o2_ref, l2_ref, gate_ref,
                  wbh_ref, wba_ref, wo_ref, g1_ref, sc2_ref, sh2_ref, lng_ref, lnb_ref,
                  x1_ref, u2_ref):
    ls = [l0_ref[0], l1_ref[0], l2_ref[0]]
    os_ = [o0_ref[0], o1_ref[0], o2_ref[0]]
    mx = jnp.maximum(jnp.maximum(ls[0], ls[1]), ls[2])
    ws = [jnp.exp(l - mx) for l in ls]
    den = ws[0] + ws[1] + ws[2]
    ya = (ws[0] * os_[0].astype(F32) + ws[1] * os_[1].astype(F32) + ws[2] * os_[2].astype(F32)) / den
    hb = lax.dot_general(yh_ref[0], wbh_ref[...], (((0,), (0,)), ((), ())),
                         preferred_element_type=F32)
    ab = jnp.dot(ya.astype(BF16), wba_ref[...], preferred_element_type=F32)
    D = hb.shape[1]
    gate = gate_ref[0]
    merged = gate[:, :D].astype(F32) * hb + gate[:, D:].astype(F32) * ab
    mix = jnp.dot(merged.astype(BF16), wo_ref[...], preferred_element_type=F32)
    r = DN_ALPHA * x_ref[0] + (1.0 + g1_ref[0]) * mix
    x1 = _layer_norm(r, lng_ref[...], lnb_ref[...])
    x1_ref[0] = x1
    u2_ref[0] = (x1 * (1.0 + sc2_ref[0]) + sh2_ref[0]).astype(BF16)


def _merge(x, yhT, attn, gate, w_bh, w_ba, w_o, g1, sc2, sh2, ln_g, ln_b):
    B, S, D = x.shape
    HW = yhT.shape[1]
    tS = 256
    tok = lambda w: pl.BlockSpec((1, tS, w), lambda b, i: (b, i, 0))
    vec = pl.BlockSpec((1, 1, D), lambda b, i: (b, 0, 0))
    const = lambda shp: pl.BlockSpec(shp, lambda b, i: (0, 0))
    attn_specs, attn_args = [], []
    for o, l in attn:
        attn_specs += [tok(ATTN_OUT), tok(ATTN_OUT)]
        attn_args += [o, l]
    return pl.pallas_call(
        _merge_kernel,
        grid=(B, S // tS),
        in_specs=[tok(D), pl.BlockSpec((1, HW, tS), lambda b, i: (b, 0, i))] + attn_specs
                 + [tok(2 * D), const((HW, D)), const((ATTN_OUT, D)), const((D, D)),
                    vec, vec, vec, const((1, D)), const((1, D))],
        out_specs=[tok(D), tok(D)],
        out_shape=[jax.ShapeDtypeStruct((B, S, D), F32), jax.ShapeDtypeStruct((B, S, D), BF16)],
        compiler_params=_cparams("parallel", "parallel"),
        name="merge",
    )(x, yhT, *attn_args, gate, w_bh.astype(BF16), w_ba.astype(BF16), w_o.astype(BF16),
      g1, sc2, sh2, ln_g.reshape(1, D), ln_b.reshape(1, D))


PEER_TS = 512
PEER_EB = 1024


def _top_values(s, k):
    vals = []
    work = s
    for a in range(k):
        m = jnp.max(work, axis=0, keepdims=True)
        vals.append(m)
        if a + 1 < k:
            work = jnp.where(work == m, -jnp.inf, work)
    return vals


def _gelu(a):
    return 0.5 * a * (1.0 + lax.erf(a * (2.0 ** -0.5)))


def _peer_route(a0, a1):
    K = PEER_TOPK
    p0 = _top_values(a0, K)
    p1 = _top_values(a1, K)
    sv0_hi = jnp.concatenate(p0[K // 2:], axis=0)
    sv1_lo = jnp.concatenate(p1[:K // 2], axis=0)
    sv1_hi = jnp.concatenate(p1[K // 2:], axis=0)
    cells = [p0[0] + sv1_lo, p0[0] + sv1_hi]
    cells += [p0[a] + sv1_lo for a in range(1, K // 2)]
    cells += [sv0_hi + p1[0]]
    cand = jnp.concatenate(cells, axis=0)
    tv = _top_values(cand, K)
    tau, top = tv[K - 1], tv[0]
    z = jnp.sum(jnp.where(cand >= tau, jnp.exp(cand - top), 0.0), axis=0, keepdims=True)
    in0 = a0 >= p0[K - 1]
    cnt = jnp.zeros_like(a0)
    rank1 = jnp.zeros_like(a1)
    for b in range(K):
        cnt = cnt + jnp.where((a0 + p1[b]) >= tau, 1.0, 0.0)
        rank1 = rank1 + jnp.where(p1[b] > a1, 1.0, 0.0)
    nb = jnp.where(in0, cnt, 0.0)
    c = jnp.where(in0, jnp.exp(a0 - p0[0]) * (1.0 / z), 0.0)
    q = jnp.where(a1 >= p1[K - 1], jnp.exp(a1 - p1[0]), 0.0)
    return c, nb, rank1, q


def _peer_kernel(u2_ref, x1_ref, g2_ref, wq_ref, keys_ref, u_ref, vt_ref, lng_ref, lnb_ref, out_ref,
                 cc_ref, nb_ref, rk_ref, qq_ref, s0_ref, s1_ref, at_ref, coef_ref, acc_ref):
    e = pl.program_id(1)
    ts = u2_ref.shape[0]
    eb = u_ref.shape[0]
    n_lt = ts // LANES
    nk = PEER_NKEYS
    nt = (((1,), (1,)), ((), ()))

    @pl.when(e == 0)
    def _prologue():
        acc_ref[...] = jnp.zeros_like(acc_ref)
        u2 = u2_ref[...]

        def head_body(h, carry):
            qt = lax.dot_general(wq_ref[h], u2, nt, preferred_element_type=F32)
            s0_ref[...] = jnp.dot(keys_ref[2 * h], qt[:PEER_KEY_DIM].astype(BF16),
                                  preferred_element_type=F32)
            s1_ref[...] = jnp.dot(keys_ref[2 * h + 1], qt[PEER_KEY_DIM:].astype(BF16),
                                  preferred_element_type=F32)
            for lt in range(n_lt):
                sl = slice(lt * LANES, (lt + 1) * LANES)
                c, nb, rank1, q = _peer_route(s0_ref[:, sl], s1_ref[:, sl])
                cc_ref[h, lt] = c
                nb_ref[h, lt] = nb
                rk_ref[h, lt] = rank1
                qq_ref[h, lt] = q
            return carry

        lax.fori_loop(0, PEER_HEADS, head_body, 0)

    at_ref[...] = lax.dot_general(u_ref[...], u2_ref[...], nt, preferred_element_type=F32)

    def gate_body(i, carry):
        ig = e * (eb // nk) + i
        r0 = pl.multiple_of(i * nk, nk)
        for lt in range(n_lt):
            sl = slice(lt * LANES, (lt + 1) * LANES)
            g = jnp.zeros((nk, LANES), F32)
            for h in range(PEER_HEADS):
                c_row = cc_ref[h, lt, pl.ds(ig, 1), :]
                nb_row = nb_ref[h, lt, pl.ds(ig, 1), :]
                g = g + c_row * jnp.where(rk_ref[h, lt] < nb_row, qq_ref[h, lt], 0.0)
            a = at_ref[pl.ds(r0, nk), sl]
            coef_ref[pl.ds(r0, nk), sl] = (_gelu(a) * g).astype(BF16)
        return carry

    lax.fori_loop(0, eb // nk, gate_body, 0)
    acc_ref[...] += jnp.dot(vt_ref[...], coef_ref[...], preferred_element_type=F32)

    @pl.when(e == pl.num_programs(1) - 1)
    def _epilogue():
        f = acc_ref[...].T
        r = DN_ALPHA * x1_ref[...] + (1.0 + g2_ref[0]) * f
        out_ref[...] = _layer_norm(r, lng_ref[...], lnb_ref[...])


def _peer(u2, x1, g2, wq, keys, utab, vtab, ln_g, ln_b, seq):
    T, D = u2.shape
    E = utab.shape[0]
    ts, eb = PEER_TS, PEER_EB
    n_lt = ts // LANES
    wqT = wq.T.astype(BF16).reshape(PEER_HEADS, 2 * PEER_KEY_DIM, D)
    keys2 = keys.astype(BF16).reshape(2 * PEER_HEADS, PEER_NKEYS, PEER_KEY_DIM)
    ub = utab.astype(BF16)
    vt = vtab.T.astype(BF16)
    tiles_per_seq = seq // ts
    route = lambda: pltpu.VMEM((PEER_HEADS, n_lt, PEER_NKEYS, LANES), F32)
    return pl.pallas_call(
        _peer_kernel,
        grid=(T // ts, E // eb),
        in_specs=[pl.BlockSpec((ts, D), lambda t, e: (t, 0)),
                  pl.BlockSpec((ts, D), lambda t, e: (t, 0)),
                  pl.BlockSpec((1, 1, D), lambda t, e: (t // tiles_per_seq, 0, 0)),
                  pl.BlockSpec((PEER_HEADS, 2 * PEER_KEY_DIM, D), lambda t, e: (0, 0, 0)),
                  pl.BlockSpec((2 * PEER_HEADS, PEER_NKEYS, PEER_KEY_DIM), lambda t, e: (0, 0, 0)),
                  pl.BlockSpec((eb, D), lambda t, e: (e, 0)),
                  pl.BlockSpec((D, eb), lambda t, e: (0, e)),
                  pl.BlockSpec((1, D), lambda t, e: (0, 0)),
                  pl.BlockSpec((1, D), lambda t, e: (0, 0))],
        out_specs=pl.BlockSpec((ts, D), lambda t, e: (t, 0)),
        out_shape=jax.ShapeDtypeStruct((T, D), F32),
        scratch_shapes=[route(), route(), route(), route(),
                        pltpu.VMEM((PEER_NKEYS, ts), F32), pltpu.VMEM((PEER_NKEYS, ts), F32),
                        pltpu.VMEM((eb, ts), F32), pltpu.VMEM((eb, ts), BF16),
                        pltpu.VMEM((D, ts), F32)],
        compiler_params=_cparams("parallel", "arbitrary"),
        name="peer",
    )(u2, x1, g2, wqT, keys2, ub, vt, ln_g.reshape(1, D), ln_b.reshape(1, D))


def kernel(x, c, positions, w_ada, b_ada, w_in, hy_conv_w, hy_conv_b, hy_f1_w, hy_f1_b, hy_f1_freq, hy_f2_w, hy_f2_b, hy_f2_freq, hy_f3_w, hy_bias, w_branch_hyena, w_branch_attn, w_out, ln1_g, ln1_b, peer_wq, peer_keys, peer_u, peer_v, ln2_g, ln2_b):
    B, S, D = x.shape
    ada = _ada(c, w_ada[0], b_ada[0])
    sh1, sc1, g1, sh2, sc2, g2 = [a.reshape(B, 1, D) for a in jnp.split(ada, 6, axis=-1)]
    phT, qkv, gate = _inproj(x, sc1, sh1, positions, w_in[0])
    yhT = _hyena(phT, hy_conv_w[0], hy_conv_b[0], hy_f1_w[0], hy_f1_b[0], hy_f1_freq[0],
                 hy_f2_w[0], hy_f2_b[0], hy_f2_freq[0], hy_f3_w[0], hy_bias[0])
    attn = [_attn_group(qkv, g) for g in range(N_GROUPS)]
    x1, u2 = _merge(x, yhT, attn, gate, w_branch_hyena[0], w_branch_attn[0], w_out[0],
                    g1, sc2, sh2, ln1_g[0], ln1_b[0])
    out = _peer(u2.reshape(B * S, D), x1.reshape(B * S, D), g2, peer_wq[0], peer_keys[0],
                peer_u[0], peer_v[0], ln2_g[0], ln2_b[0], S)
    return out.reshape(B, S, D)
```

```python
import functools
import math

import numpy as np
import jax
import jax.numpy as jnp
from jax import lax
from jax.experimental import pallas as pl
from jax.experimental.pallas import tpu as pltpu

F32 = jnp.float32
BF16 = jnp.bfloat16

D_MODEL = 1024
HEAD_DIM = 64
HEADS_PER_GROUP = 4
DILATIONS = (1, 4, 16)
ATTN_HALF = 64
N_GROUPS = 3
ATTN_WIDTH = N_GROUPS * HEADS_PER_GROUP * HEAD_DIM
ATTN_OUT = HEADS_PER_GROUP * HEAD_DIM
ROPE_THETA = 500000.0
ROPE_DIM = HEAD_DIM // 4
NEG_BIG = -1e30
HYENA_WIDTH = D_MODEL // 2
HYENA_BANDS = 16
HYENA_HIDDEN = 64
DECAY_SHORT_PCT = 0.3
DECAY_LONG_PCT = 1.5
DECAY_TARGET = 1e-2
PEER_HEADS = 8
PEER_NKEYS = 128
PEER_KEY_DIM = 128
PEER_TOPK = 16
DN_ALPHA = 2.0 ** 0.25
LN_EPS = 1e-5

LANES = 128
VMEM_LIMIT = 56 * 1024 * 1024

HIGHEST = lax.Precision.HIGHEST


def _cparams(*sem, flags=None):
    return pltpu.CompilerParams(dimension_semantics=sem, vmem_limit_bytes=VMEM_LIMIT, flags=flags)


def _ada_kernel(c_ref, w_ref, b_ref, o_ref):
    c = c_ref[...]
    s = c * jax.nn.sigmoid(c)
    o_ref[...] = jnp.dot(s, w_ref[...], precision=HIGHEST, preferred_element_type=F32) + b_ref[...]


def _ada(c, w_ada, b_ada):
    B, D = c.shape
    N = w_ada.shape[1]
    tn = 1024
    return pl.pallas_call(
        _ada_kernel,
        grid=(N // tn,),
        in_specs=[pl.BlockSpec((B, D), lambda j: (0, 0)),
                  pl.BlockSpec((D, tn), lambda j: (0, j)),
                  pl.BlockSpec((1, tn), lambda j: (0, j))],
        out_specs=pl.BlockSpec((B, tn), lambda j: (0, j)),
        out_shape=jax.ShapeDtypeStruct((B, N), F32),
        compiler_params=_cparams("parallel"),
        name="ada",
    )(c, w_ada, b_ada.reshape(1, N))


def _deinterleave_matrix(n, d, transpose=False):
    row = lax.broadcasted_iota(jnp.int32, (n, n), 0)
    col = lax.broadcasted_iota(jnp.int32, (n, n), 1)
    if transpose:
        row, col = col, row
    per = n // d
    return col == (row % per) * d + row // per


def _inproj_kernel(x_ref, sc_ref, sh_ref, pos_ref, inv_ref, whT_ref, wqkv_ref, wg_ref,
                   phT_ref, qkv0_ref, qkv1_ref, qkv2_ref, gate_ref):
    u = x_ref[0] * (1.0 + sc_ref[0]) + sh_ref[0]
    ub = u.astype(BF16)
    tS = ub.shape[0]
    phT_ref[0] = lax.dot_general(whT_ref[...], ub, (((1,), (1,)), ((), ())),
                                 preferred_element_type=F32)
    qkv = jnp.dot(ub, wqkv_ref[...], preferred_element_type=F32)
    ang = pos_ref[0].astype(F32) * inv_ref[...]
    cos_a = jnp.cos(ang)
    sin_a = jnp.sin(ang)
    lane = lax.broadcasted_iota(jnp.int32, (1, LANES), 1) % HEAD_DIM
    half = ROPE_DIM // 2
    s_lo = jnp.where(lane < half, -sin_a, 0.0)
    s_hi = jnp.where((lane >= half) & (lane < ROPE_DIM), sin_a, 0.0)
    n_part = ATTN_WIDTH // LANES
    tiles = []
    for j in range(3 * n_part):
        t = qkv[:, j * LANES:(j + 1) * LANES]
        if j < 2 * n_part:
            t = t * cos_a + pltpu.roll(t, half, 1) * s_hi + pltpu.roll(t, LANES - half, 1) * s_lo
        if j < n_part:
            t = t * (HEAD_DIM ** -0.5)
        tiles.append(t.astype(BF16))
    per_g = ATTN_OUT // LANES
    for g, (d, o_ref) in enumerate(zip(DILATIONS, (qkv0_ref, qkv1_ref, qkv2_ref))):
        xg = jnp.concatenate([tiles[p * n_part + g * per_g + k] for p in range(3) for k in range(per_g)],
                             axis=1)
        if d == 1:
            o_ref[0] = xg
        else:
            perm = jnp.where(_deinterleave_matrix(tS, d), 1.0, 0.0).astype(BF16)
            z = jnp.dot(perm, xg, preferred_element_type=F32).astype(BF16)
            w = xg.shape[1]
            for r in range(d):
                o_ref[0, :, r * w:(r + 1) * w] = z[r * (tS // d):(r + 1) * (tS // d)]
    g = jnp.dot(ub, wg_ref[...], preferred_element_type=F32)
    gate_ref[0] = jax.nn.sigmoid(g).astype(BF16)


def _rope_lane_inv():
    inv = ROPE_THETA ** (-jnp.arange(0, ROPE_DIM, 2, dtype=F32) / ROPE_DIM)
    lane = np.arange(LANES) % HEAD_DIM
    idx = lane % (ROPE_DIM // 2)
    return jnp.where(jnp.asarray(lane < ROPE_DIM), inv[idx], 0.0).reshape(1, LANES).astype(F32)


def _inproj(x, sc1, sh1, positions, w_in):
    B, S, D = x.shape
    hy_cols = 3 * HYENA_WIDTH
    qkv_cols = 3 * ATTN_WIDTH
    whT = w_in[:, :hy_cols].T.astype(BF16)
    wqkv = w_in[:, hy_cols:hy_cols + qkv_cols].astype(BF16)
    wg = w_in[:, hy_cols + qkv_cols:].astype(BF16)
    n_gate = wg.shape[1]
    tS = 256
    gw = 3 * ATTN_OUT
    const = lambda b, i: (0, 0)
    return pl.pallas_call(
        _inproj_kernel,
        grid=(B, S // tS),
        in_specs=[pl.BlockSpec((1, tS, D), lambda b, i: (b, i, 0)),
                  pl.BlockSpec((1, 1, D), lambda b, i: (b, 0, 0)),
                  pl.BlockSpec((1, 1, D), lambda b, i: (b, 0, 0)),
                  pl.BlockSpec((1, tS, 1), lambda b, i: (b, i, 0)),
                  pl.BlockSpec((1, LANES), const),
                  pl.BlockSpec((hy_cols, D), const),
                  pl.BlockSpec((D, qkv_cols), const),
                  pl.BlockSpec((D, n_gate), const)],
        out_specs=[pl.BlockSpec((1, hy_cols, tS), lambda b, i: (b, 0, i))]
                  + [pl.BlockSpec((1, tS // d, d * gw), lambda b, i: (b, i, 0)) for d in DILATIONS]
                  + [pl.BlockSpec((1, tS, n_gate), lambda b, i: (b, i, 0))],
        out_shape=[jax.ShapeDtypeStruct((B, hy_cols, S), F32)]
                  + [jax.ShapeDtypeStruct((B, S // d, d * gw), BF16) for d in DILATIONS]
                  + [jax.ShapeDtypeStruct((B, S, n_gate), BF16)],
        compiler_params=_cparams("parallel", "parallel"),
        name="inproj",
    )(x, sc1, sh1, positions.reshape(B, S, 1), _rope_lane_inv(), whT, wqkv, wg)


ATTN_QSUB = 128
ATTN_KWIN = 256


def _attn_kernel(q_ref, k_ref, v_ref, o_ref, lse_ref, *, n, qb):
    i = pl.program_id(2)
    head_of_lane = lax.broadcasted_iota(jnp.int32, (1, ATTN_OUT), 1) // HEAD_DIM
    for j in range(qb // ATTN_QSUB):
        q0 = i * qb + j * ATTN_QSUB
        ks = pl.multiple_of(jnp.clip(q0 - ATTN_HALF, 0, n - ATTN_KWIN), ATTN_HALF)
        kw = k_ref[0, pl.ds(ks, ATTN_KWIN), :]
        vw = v_ref[0, pl.ds(ks, ATTN_KWIN), :]
        qt = q_ref[0, j * ATTN_QSUB:(j + 1) * ATTN_QSUB, :]
        qi = q0 + lax.broadcasted_iota(jnp.int32, (ATTN_QSUB, 1), 0)
        kj = ks + lax.broadcasted_iota(jnp.int32, (1, ATTN_KWIN), 1)
        valid = jnp.abs(qi - kj) <= ATTN_HALF
        heads = range(HEADS_PER_GROUP)
        hms = [head_of_lane == h for h in heads]
        ss = [lax.dot_general(jnp.where(hm, qt, jnp.zeros_like(qt)), kw, (((1,), (1,)), ((), ())),
                              preferred_element_type=F32) for hm in hms]
        ss = [jnp.where(valid, s, NEG_BIG) for s in ss]
        ms = [jnp.max(s, axis=1, keepdims=True) for s in ss]
        ps = [jnp.exp(s - m) for s, m in zip(ss, ms)]
        ls = [jnp.sum(p, axis=1, keepdims=True) for p in ps]
        os_ = [jnp.dot(p.astype(BF16), vw, preferred_element_type=F32) for p in ps]
        acc = jnp.zeros((ATTN_QSUB, ATTN_OUT), F32)
        lse = jnp.zeros((ATTN_QSUB, ATTN_OUT), F32)
        for hm, o, m, l in zip(hms, os_, ms, ls):
            acc = jnp.where(hm, o * (1.0 / l), acc)
            lse = jnp.where(hm, m + jnp.log(l), lse)
        o_ref[0, j * ATTN_QSUB:(j + 1) * ATTN_QSUB, :] = acc.astype(o_ref.dtype)
        lse_ref[0, j * ATTN_QSUB:(j + 1) * ATTN_QSUB, :] = lse


def _attn_group(view, g):
    B, n, W = view.shape
    d = DILATIONS[g]
    assert W == d * 3 * ATTN_OUT
    qb = min(512, n)
    kern = functools.partial(_attn_kernel, n=n, qb=qb)
    return pl.pallas_call(
        kern,
        grid=(B, d, n // qb),
        in_specs=[pl.BlockSpec((1, qb, ATTN_OUT), lambda b, r, i: (b, i, 3 * r)),
                  pl.BlockSpec((1, n, ATTN_OUT), lambda b, r, i: (b, 0, 3 * r + 1)),
                  pl.BlockSpec((1, n, ATTN_OUT), lambda b, r, i: (b, 0, 3 * r + 2))],
        out_specs=[pl.BlockSpec((1, qb, ATTN_OUT), lambda b, r, i: (b, i, r)),
                   pl.BlockSpec((1, qb, ATTN_OUT), lambda b, r, i: (b, i, r))],
        out_shape=[jax.ShapeDtypeStruct((B, n, d * ATTN_OUT), BF16),
                   jax.ShapeDtypeStruct((B, n, d * ATTN_OUT), F32)],
        compiler_params=_cparams("parallel", "parallel", "arbitrary"),
        name=f"attn_d{d}",
    )(view, view, view)


FFT_R = 128
HYENA_GROUP = 4


def _dft_consts():
    n = np.arange(FFT_R)
    ang = 2.0 * np.pi * np.outer(n, n) / FFT_R
    fr, fi = np.cos(ang), -np.sin(ang)
    tw = 2.0 * np.pi * np.outer(n, n) / (FFT_R * FFT_R)
    twr, twi = np.cos(tw), -np.sin(tw)
    inv_n = 1.0 / (FFT_R * FFT_R)
    half = FFT_R // 2
    c = dict(
        f2=np.concatenate([fr, fi], 0),
        w3=np.block([[fr, fi], [-fi, fr]]),
        w1c=np.block([[fr, -fi], [fi, fr]]),
        c2=np.concatenate([fr[:half] * inv_n, -fi[:half] * inv_n], 0),
    )
    out = {k: jnp.asarray(v, F32).astype(BF16) for k, v in c.items()}
    out["twr"] = jnp.asarray(twr, F32)
    out["twi"] = jnp.asarray(twi, F32)
    return out


def _filt_mlp_kernel(tt_ref, ww_ref, f_ref, w1t_ref, w1c_ref, w1s_ref, b1_ref, fr1_ref,
                     w2_ref, b2_ref, fr2_ref, o_ref):
    t = tt_ref[0]
    fw = f_ref[...] * ww_ref[0]
    pre = (w1t_ref[...] * t
           + jnp.dot(w1c_ref[...], jnp.cos(fw), precision=HIGHEST, preferred_element_type=F32)
           + jnp.dot(w1s_ref[...], -jnp.sin(fw), precision=HIGHEST, preferred_element_type=F32))
    h1 = jnp.sin(fr1_ref[...] * (pre + b1_ref[...]))
    h2 = jnp.dot(w2_ref[...], h1, precision=HIGHEST, preferred_element_type=F32) + b2_ref[...]
    o_ref[0] = jnp.sin(fr2_ref[...] * h2)


def _filt_time_kernel(h2_ref, w3_ref, tt_ref, ad_ref, o_ref):
    half = pl.program_id(2)
    h = jnp.dot(w3_ref[0, 0], h2_ref[0], precision=HIGHEST, preferred_element_type=F32)
    decay = jnp.exp(-tt_ref[0] * ad_ref[...])
    lane = lax.broadcasted_iota(jnp.int32, (1, h.shape[1]), 1)
    o_ref[0] = jnp.where((lane == 0) & (half == 1), 0.0, h * decay)


def _filt_spec_kernel(k_ref, f2_ref, twr_ref, twi_ref, w3_ref, o_ref, *, cf):
    twr = twr_ref[...]
    twi = twi_ref[...]

    for g0 in range(0, cf, HYENA_GROUP):
        cs = range(g0, g0 + HYENA_GROUP)
        kt = jnp.concatenate([k_ref[0, c] for c in cs], axis=1).astype(BF16)
        m = jnp.dot(f2_ref[...], kt, preferred_element_type=F32)
        a_rows = []
        for k in range(HYENA_GROUP):
            ar = m[:FFT_R, k * FFT_R:(k + 1) * FFT_R]
            ai = m[FFT_R:, k * FFT_R:(k + 1) * FFT_R]
            a_rows.append(jnp.concatenate([ar * twr - ai * twi, ar * twi + ai * twr], axis=1))
        xs = jnp.dot(jnp.concatenate(a_rows, axis=0).astype(BF16), w3_ref[...],
                     preferred_element_type=F32)
        for k, c in enumerate(cs):
            o_ref[0, c] = xs[k * FFT_R:(k + 1) * FFT_R]


def _hyena_conv_kernel(cw_ref, cb_ref, hb_ref, v_ref, x1_ref, x2_ref, ks_ref,
                       f2_ref, twr_ref, twi_ref, w3_ref, w1c_ref, c2_ref, o_ref, *, ct, nb):
    ci = pl.program_id(0)
    rows = FFT_R // 2
    twr = twr_ref[...]
    twi = twi_ref[...]
    lane = lax.broadcasted_iota(jnp.int32, (rows, LANES), 1)
    row = lax.broadcasted_iota(jnp.int32, (rows, LANES), 0)
    n_ch = cb_ref.shape[0]
    width = hb_ref.shape[0] // 2

    def short_conv(z, ch):
        r = pltpu.roll(z, 1, 1)
        r2 = pltpu.roll(r, 1, 0)
        prev = jnp.where(lane == 0, jnp.where(row == 0, 0.0, r2), r)
        f = pltpu.roll(z, LANES - 1, 1)
        f2 = pltpu.roll(f, rows - 1, 0)
        nxt = jnp.where(lane == LANES - 1, jnp.where(row == rows - 1, 0.0, f2), f)
        return prev * cw_ref[ch] + z * cw_ref[n_ch + ch] + nxt * cw_ref[2 * n_ch + ch] + cb_ref[ch]

    W2 = 2 * FFT_R

    def long_conv(zs, ks):
        n = len(zs)
        zc = jnp.concatenate([z for pair in zs for z in pair], axis=1).astype(BF16)
        m = jnp.dot(f2_ref[...], zc, preferred_element_type=F32)
        a_rows = []
        for c in range(n):
            mc = m[:, c * W2:(c + 1) * W2]
            ar = mc[:FFT_R, :FFT_R] - mc[FFT_R:, FFT_R:]
            ai = mc[FFT_R:, :FFT_R] + mc[:FFT_R, FFT_R:]
            a_rows.append(jnp.concatenate([ar * twr - ai * twi, ar * twi + ai * twr], axis=1))
        xs = jnp.dot(jnp.concatenate(a_rows, axis=0).astype(BF16), w3_ref[...],
                     preferred_element_type=F32)
        y_rows = []
        for c in range(n):
            xc = xs[c * FFT_R:(c + 1) * FFT_R]
            xr, xi = xc[:, :FFT_R], xc[:, FFT_R:]
            kr, ki = ks[c][:, :FFT_R], ks[c][:, FFT_R:]
            y_rows.append(jnp.concatenate([xr * kr - xi * ki, xr * ki + xi * kr], axis=1))
        bs = jnp.dot(jnp.concatenate(y_rows, axis=0).astype(BF16), w1c_ref[...],
                     preferred_element_type=F32)
        b_cols = []
        for c in range(n):
            bc = bs[c * FFT_R:(c + 1) * FFT_R]
            br, bi = bc[:, :FFT_R], bc[:, FFT_R:]
            b_cols += [br * twr + bi * twi, bi * twr - br * twi]
        m4 = jnp.dot(c2_ref[...], jnp.concatenate(b_cols, axis=1).astype(BF16),
                     preferred_element_type=F32)
        out = []
        for c in range(n):
            mc = m4[:, c * W2:(c + 1) * W2]
            out.append([mc[:rows, :FFT_R] - mc[rows:, FFT_R:], mc[:rows, FFT_R:] + mc[rows:, :FFT_R]])
        return out

    def body(gi, carry):
        cs = [gi * HYENA_GROUP + k for k in range(HYENA_GROUP)]
        z = [[short_conv(v_ref[b, c], ci * ct + c) for b in range(nb)] for c in cs]
        for o, g_ref in enumerate((x1_ref, x2_ref)):
            y = long_conv(z, [ks_ref[o, c] for c in cs])
            z_new = []
            for k, c in enumerate(cs):
                ch = ci * ct + c
                bias = hb_ref[o * width + ch]
                gch = (o + 1) * width + ch
                z_new.append([short_conv(g_ref[b, c], gch) * (y[k][b] + z[k][b] * bias)
                              for b in range(nb)])
            z = z_new
        for k, c in enumerate(cs):
            for b in range(nb):
                o_ref[b, c] = z[k][b].astype(o_ref.dtype)
        return carry

    for gi in range(ct // HYENA_GROUP):
        body(gi, 0)


def _hyena(phT, conv_w, conv_b, f1_w, f1_b, f1_freq, f2_w, f2_b, f2_freq, f3_w, bias):
    B, C3, L = phT.shape
    assert B == 2 and 2 * L == FFT_R * FFT_R
    HW = C3 // 3
    hid = f2_w.shape[0]
    cs = _dft_consts()
    t = jnp.linspace(0.0, 1.0, L, dtype=F32)
    w = 2.0 * math.pi * jnp.arange(L, dtype=F32) / L
    rev = lambda a: jnp.roll(a[::-1], 1)
    tt = jnp.stack([t, rev(t)]).reshape(2, 1, L)
    ww = jnp.stack([w, rev(w)]).reshape(2, 1, L)
    bands = jnp.linspace(1e-4, HYENA_BANDS - 1, HYENA_BANDS, dtype=F32).reshape(HYENA_BANDS, 1)
    max_decay = math.log(DECAY_TARGET) / DECAY_SHORT_PCT
    min_decay = math.log(DECAY_TARGET) / DECAY_LONG_PCT
    absdelta = jnp.abs(jnp.linspace(min_decay, max_decay, HW, dtype=F32)).reshape(HW, 1)

    w1T = f1_w.T.astype(F32)
    col = lambda a: a.astype(F32).reshape(-1, 1)
    full = lambda shp: pl.BlockSpec(shp, lambda *a: (0,) * len(shp))
    h2 = pl.pallas_call(
        _filt_mlp_kernel,
        grid=(2,),
        in_specs=[pl.BlockSpec((1, 1, L), lambda h: (h, 0, 0)),
                  pl.BlockSpec((1, 1, L), lambda h: (h, 0, 0)),
                  full((HYENA_BANDS, 1)), full((hid, 1)), full((hid, HYENA_BANDS)),
                  full((hid, HYENA_BANDS)), full((hid, 1)), full((hid, 1)),
                  full((hid, hid)), full((hid, 1)), full((hid, 1))],
        out_specs=pl.BlockSpec((1, hid, L), lambda h: (h, 0, 0)),
        out_shape=jax.ShapeDtypeStruct((2, hid, L), F32),
        compiler_params=_cparams("parallel"),
        name="hyena_filter_mlp",
    )(tt, ww, bands, w1T[:, :1], w1T[:, 1:1 + HYENA_BANDS], w1T[:, 1 + HYENA_BANDS:],
      col(f1_b), col(f1_freq), f2_w.T.astype(F32), col(f2_b), col(f2_freq))

    w3T = f3_w.T.astype(F32).reshape(2, 2, HW, hid)
    ctf = 128
    kt = pl.pallas_call(
        _filt_time_kernel,
        grid=(2, HW // ctf, 2),
        in_specs=[pl.BlockSpec((1, hid, L), lambda o, c, h: (h, 0, 0)),
                  pl.BlockSpec((1, 1, ctf, hid), lambda o, c, h: (h, o, c, 0)),
                  pl.BlockSpec((1, 1, L), lambda o, c, h: (h, 0, 0)),
                  pl.BlockSpec((ctf, 1), lambda o, c, h: (c, 0))],
        out_specs=pl.BlockSpec((1, ctf, L), lambda o, c, h: (o, c, h)),
        out_shape=jax.ShapeDtypeStruct((2, HW, 2 * L), F32),
        compiler_params=_cparams("parallel", "parallel", "arbitrary"),
        name="hyena_filter_time",
    )(h2, w3T, tt, absdelta)

    cf = 16
    kspec = pl.pallas_call(
        functools.partial(_filt_spec_kernel, cf=cf),
        grid=(2, HW // cf),
        in_specs=[pl.BlockSpec((1, cf, FFT_R, FFT_R), lambda o, c: (o, c, 0, 0)),
                  full((2 * FFT_R, FFT_R)), full((FFT_R, FFT_R)), full((FFT_R, FFT_R)),
                  full((2 * FFT_R, 2 * FFT_R))],
        out_specs=pl.BlockSpec((1, cf, FFT_R, 2 * FFT_R), lambda o, c: (o, c, 0, 0)),
        out_shape=jax.ShapeDtypeStruct((2, HW, FFT_R, 2 * FFT_R), F32),
        compiler_params=_cparams("parallel", "parallel"),
        name="hyena_filter_spec",
    )(kt.reshape(2, HW, FFT_R, FFT_R), cs["f2"], cs["twr"], cs["twi"], cs["w3"])

    ct = 8
    rows = FFT_R // 2
    ph4 = phT.reshape(B, C3, rows, LANES)
    nblk = HW // ct
    smem = pl.BlockSpec(memory_space=pltpu.SMEM)
    yh = pl.pallas_call(
        functools.partial(_hyena_conv_kernel, ct=ct, nb=B),
        grid=(nblk,),
        in_specs=[smem, smem, smem,
                  pl.BlockSpec((B, ct, rows, LANES), lambda c: (0, c, 0, 0)),
                  pl.BlockSpec((B, ct, rows, LANES), lambda c: (0, c + nblk, 0, 0)),
                  pl.BlockSpec((B, ct, rows, LANES), lambda c: (0, c + 2 * nblk, 0, 0)),
                  pl.BlockSpec((2, ct, FFT_R, 2 * FFT_R), lambda c: (0, c, 0, 0)),
                  full((2 * FFT_R, rows)), full((FFT_R, FFT_R)), full((FFT_R, FFT_R)),
                  full((2 * FFT_R, 2 * FFT_R)), full((2 * FFT_R, 2 * FFT_R)), full((FFT_R, FFT_R))],
        out_specs=pl.BlockSpec((B, ct, rows, LANES), lambda c: (0, c, 0, 0)),
        out_shape=jax.ShapeDtypeStruct((B, HW, rows, LANES), BF16),
        compiler_params=_cparams("parallel"),
        name="hyena_conv",
    )(conv_w.astype(F32).reshape(-1), conv_b.astype(F32), bias.astype(F32).reshape(-1),
      ph4, ph4, ph4, kspec, cs["f2"][:, :rows], cs["twr"], cs["twi"], cs["w3"], cs["w1c"], cs["c2"])
    return yh.reshape(B, HW, L)


def _layer_norm(r, g, b):
    mu = jnp.mean(r, axis=-1, keepdims=True)
    d = r - mu
    var = jnp.mean(d * d, axis=-1, keepdims=True)
    return d * lax.rsqrt(var + LN_EPS) * g + b


def _tokens_from_rows(blk, d, is_f32):
    if d == 1:
        return blk
    w = blk.shape[1] // d
    z = jnp.concatenate([blk[:, r * w:(r + 1) * w] for r in range(d)], axis=0)
    perm = jnp.where(_deinterleave_matrix(z.shape[0], d, transpose=True), 1.0, 0.0)
    if is_f32:
        return jnp.dot(perm, z, precision=HIGHEST, preferred_element_type=F32)
    return jnp.dot(perm.astype(BF16), z, preferred_element_type=F32)


def _merge_kernel(x_ref, yh_ref, o0_ref, l0_ref, o1_ref, l1_ref, o2_ref, l2_ref, gate_ref,
                  wbh_ref, wba_ref, wo_ref, g1_ref, sc2_ref, sh2_ref, lng_ref, lnb_ref,
                  x1_ref, u2_ref):
    ls = [_tokens_from_rows(r[0], d, True) for r, d in zip((l0_ref, l1_ref, l2_ref), DILATIONS)]
    os_ = [_tokens_from_rows(r[0], d, False) for r, d in zip((o0_ref, o1_ref, o2_ref), DILATIONS)]
    mx = jnp.maximum(jnp.maximum(ls[0], ls[1]), ls[2])
    ws = [jnp.exp(l - mx) for l in ls]
    den = ws[0] + ws[1] + ws[2]
    ya = (ws[0] * os_[0].astype(F32) + ws[1] * os_[1].astype(F32) + ws[2] * os_[2].astype(F32)) / den
    hb = lax.dot_general(yh_ref[0], wbh_ref[...], (((0,), (0,)), ((), ())),
                         preferred_element_type=F32)
    ab = jnp.dot(ya.astype(BF16), wba_ref[...], preferred_element_type=F32)
    D = hb.shape[1]
    gate = gate_ref[0]
    merged = gate[:, :D].astype(F32) * hb + gate[:, D:].astype(F32) * ab
    mix = jnp.dot(merged.astype(BF16), wo_ref[...], preferred_element_type=F32)
    r = DN_ALPHA * x_ref[0] + (1.0 + g1_ref[0]) * mix
    x1 = _layer_norm(r, lng_ref[...], lnb_ref[...])
    x1_ref[0] = x1
    u2_ref[0] = (x1 * (1.0 + sc2_ref[0]) + sh2_ref[0]).astype(BF16)


def _merge(x, yhT, attn, gate, w_bh, w_ba, w_o, g1, sc2, sh2, ln_g, ln_b):
    B, S, D = x.shape
    HW = yhT.shape[1]
    tS = 256
    tok = lambda w: pl.BlockSpec((1, tS, w), lambda b, i: (b, i, 0))
    vec = pl.BlockSpec((1, 1, D), lambda b, i: (b, 0, 0))
    const = lambda shp: pl.BlockSpec(shp, lambda b, i: (0, 0))
    attn_specs, attn_args = [], []
    for (o, l), d in zip(attn, DILATIONS):
        spec = pl.BlockSpec((1, tS // d, d * ATTN_OUT), lambda b, i: (b, i, 0))
        attn_specs += [spec, spec]
        attn_args += [o, l]
    return pl.pallas_call(
        _merge_kernel,
        grid=(B, S // tS),
        in_specs=[tok(D), pl.BlockSpec((1, HW, tS), lambda b, i: (b, 0, i))] + attn_specs
                 + [tok(2 * D), const((HW, D)), const((ATTN_OUT, D)), const((D, D)),
                    vec, vec, vec, const((1, D)), const((1, D))],
        out_specs=[tok(D), tok(D)],
        out_shape=[jax.ShapeDtypeStruct((B, S, D), F32), jax.ShapeDtypeStruct((B, S, D), BF16)],
        compiler_params=_cparams("parallel", "parallel"),
        name="merge",
    )(x, yhT, *attn_args, gate, w_bh.astype(BF16), w_ba.astype(BF16), w_o.astype(BF16),
      g1, sc2, sh2, ln_g.reshape(1, D), ln_b.reshape(1, D))


PEER_TS = 512
PEER_EB = 1024
PEER_CHUNK = 256


def _top_values(s, k):
    vals = []
    work = s
    for a in range(k):
        m = jnp.max(work, axis=0, keepdims=True)
        vals.append(m)
        if a + 1 < k:
            work = jnp.where(work == m, -jnp.inf, work)
    return vals


SUBLANES = 8


def _bitonic_desc(v, merge_only=False):
    n = len(v)
    v = list(v)
    k = n if merge_only else 2
    while k <= n:
        j = k // 2
        while j >= 1:
            for i in range(n):
                l = i ^ j
                if l > i:
                    hi, lo = jnp.maximum(v[i], v[l]), jnp.minimum(v[i], v[l])
                    v[i], v[l] = (hi, lo) if (i & k) == 0 else (lo, hi)
            j //= 2
        k *= 2
    return v


def _top_sorted(s, k):
    assert s.shape[0] == k * SUBLANES
    v = _bitonic_desc([s[SUBLANES * r:SUBLANES * (r + 1)] for r in range(k)])
    shift = SUBLANES // 2
    while shift >= 1:
        w = [pltpu.roll(x, shift, 0) for x in v]
        v = _bitonic_desc([jnp.maximum(v[r], w[k - 1 - r]) for r in range(k)], merge_only=True)
        shift //= 2
    return [x[:1] for x in v]


def _gelu(a):
    return 0.5 * a * (1.0 + lax.erf(a * (2.0 ** -0.5)))


def _peer_route(a0, a1):
    K = PEER_TOPK
    p0 = _top_sorted(a0, K)
    p1 = _top_sorted(a1, K)
    sv0_hi = jnp.concatenate(p0[K // 2:], axis=0)
    sv1_lo = jnp.concatenate(p1[:K // 2], axis=0)
    sv1_hi = jnp.concatenate(p1[K // 2:], axis=0)
    cells = [p0[0] + sv1_lo, p0[0] + sv1_hi]
    cells += [p0[a] + sv1_lo for a in range(1, K // 2)]
    cells += [sv0_hi + p1[0]]
    cand = jnp.concatenate(cells, axis=0)
    tv = _top_values(cand, K)
    tau, top = tv[K - 1], tv[0]
    z = jnp.sum(jnp.where(cand >= tau, jnp.exp(cand - top), 0.0), axis=0, keepdims=True)
    in0 = a0 >= p0[K - 1]
    cnt = jnp.zeros_like(a0)
    rank1 = jnp.zeros_like(a1)
    for b in range(K):
        cnt = cnt + jnp.where((a0 + p1[b]) >= tau, 1.0, 0.0)
        rank1 = rank1 + jnp.where(p1[b] > a1, 1.0, 0.0)
    nb = jnp.where(in0, cnt, 0.0)
    c = jnp.where(in0, jnp.exp(a0 - p0[0]) * (1.0 / z), 0.0)
    q = jnp.where(a1 >= p1[K - 1], jnp.exp(a1 - p1[0]), 0.0)
    return c, nb, rank1, q


def _peer_kernel(u2_ref, x1_ref, g2_ref, wq_ref, keys_ref, u_ref, vt_ref, lng_ref, lnb_ref, out_ref,
                 cc_ref, nb_ref, rk_ref, qq_ref, s0_ref, s1_ref, at_ref, cf_ref, acc_ref, u2t_ref,
                 ccs_ref, nbs_ref):
    e = pl.program_id(1)
    ts = u2_ref.shape[0]
    eb = u_ref.shape[0]
    nk = PEER_NKEYS
    n_i = eb // nk
    n_lt = ts // LANES

    @pl.when(e == 0)
    def _prologue():
        acc_ref[...] = jnp.zeros_like(acc_ref)
        u2t_ref[...] = u2_ref[...].astype(F32).T.astype(BF16)

        def head_body(h, carry):
            qt = jnp.dot(wq_ref[h], u2t_ref[...], preferred_element_type=F32)
            s0_ref[...] = jnp.dot(keys_ref[2 * h], qt[:PEER_KEY_DIM].astype(BF16),
                                  preferred_element_type=F32)
            s1_ref[...] = jnp.dot(keys_ref[2 * h + 1], qt[PEER_KEY_DIM:].astype(BF16),
                                  preferred_element_type=F32)
            for lt in range(n_lt):
                sl = slice(lt * LANES, (lt + 1) * LANES)
                c, nb, rank1, q = _peer_route(s0_ref[:, sl], s1_ref[:, sl])
                cc_ref[h, lt] = c
                nb_ref[h, lt] = nb
                rk_ref[h, :, sl] = rank1.astype(BF16)
                qq_ref[h, :, sl] = q.astype(BF16)
            return carry

        lax.fori_loop(0, PEER_HEADS, head_body, 0)

    i0 = pl.multiple_of(e * n_i, n_i)
    for h in range(PEER_HEADS):
        for lt in range(n_lt):
            ccs_ref[h, lt] = cc_ref[h, lt, pl.ds(i0, n_i), :]
            nbs_ref[h, lt] = nb_ref[h, lt, pl.ds(i0, n_i), :]

    per = PEER_CHUNK // nk

    def stage_a(m):
        ms = slice(m * PEER_CHUNK, (m + 1) * PEER_CHUNK)
        at_ref[ms, :] = jnp.dot(u_ref[ms, :], u2t_ref[...], preferred_element_type=F32)

    def stage_b_tile(i, lt):
        rows = slice(i * nk, (i + 1) * nk)
        sl = slice(lt * LANES, (lt + 1) * LANES)
        g = jnp.zeros((nk, LANES), BF16)
        for h in range(PEER_HEADS):
            c_row = ccs_ref[h, lt, i:i + 1, :].astype(BF16)
            nb_row = nbs_ref[h, lt, i:i + 1, :].astype(BF16)
            sel = jnp.where(rk_ref[h, :, sl] < nb_row, qq_ref[h, :, sl], jnp.zeros((), BF16))
            g = g + c_row * sel
        cf_ref[rows, sl] = _gelu(at_ref[rows, sl]).astype(BF16) * g

    stage_a(0)
    for m in range(eb // PEER_CHUNK):
        for i in range(m * per, (m + 1) * per):
            for lt in range(n_lt):
                stage_b_tile(i, lt)
                if i == m * per and lt == 0 and (m + 1) * PEER_CHUNK < eb:
                    stage_a(m + 1)
        ms = slice(m * PEER_CHUNK, (m + 1) * PEER_CHUNK)
        acc_ref[...] += jnp.dot(vt_ref[:, ms], cf_ref[ms, :], preferred_element_type=F32)

    @pl.when(e == pl.num_programs(1) - 1)
    def _epilogue():
        f = acc_ref[...].T
        r = DN_ALPHA * x1_ref[...] + (1.0 + g2_ref[0]) * f
        out_ref[...] = _layer_norm(r, lng_ref[...], lnb_ref[...])


def _peer(u2, x1, g2, wq, keys, utab, vtab, ln_g, ln_b, seq):
    T, D = u2.shape
    E = utab.shape[0]
    ts, eb = PEER_TS, PEER_EB
    assert eb % PEER_CHUNK == 0 and PEER_CHUNK % PEER_NKEYS == 0
    wqT = wq.T.astype(BF16).reshape(PEER_HEADS, 2 * PEER_KEY_DIM, D)
    keys2 = keys.astype(BF16).reshape(2 * PEER_HEADS, PEER_NKEYS, PEER_KEY_DIM)
    ub = utab.astype(BF16)
    vt = vtab.T.astype(BF16)
    tiles_per_seq = seq // ts
    route = pltpu.VMEM((PEER_HEADS, PEER_NKEYS, ts), BF16)
    rows = pltpu.VMEM((PEER_HEADS, ts // LANES, PEER_NKEYS, LANES), F32)
    stage = pltpu.VMEM((PEER_HEADS, ts // LANES, eb // PEER_NKEYS, LANES), F32)
    return pl.pallas_call(
        _peer_kernel,
        grid=(T // ts, E // eb),
        in_specs=[pl.BlockSpec((ts, D), lambda t, e: (t, 0)),
                  pl.BlockSpec((ts, D), lambda t, e: (t, 0)),
                  pl.BlockSpec((1, 1, D), lambda t, e: (t // tiles_per_seq, 0, 0)),
                  pl.BlockSpec((PEER_HEADS, 2 * PEER_KEY_DIM, D), lambda t, e: (0, 0, 0)),
                  pl.BlockSpec((2 * PEER_HEADS, PEER_NKEYS, PEER_KEY_DIM), lambda t, e: (0, 0, 0)),
                  pl.BlockSpec((eb, D), lambda t, e: (e, 0)),
                  pl.BlockSpec((D, eb), lambda t, e: (0, e)),
                  pl.BlockSpec((1, D), lambda t, e: (0, 0)),
                  pl.BlockSpec((1, D), lambda t, e: (0, 0))],
        out_specs=pl.BlockSpec((ts, D), lambda t, e: (t, 0)),
        out_shape=jax.ShapeDtypeStruct((T, D), F32),
        scratch_shapes=[rows, rows, route, route,
                        pltpu.VMEM((PEER_NKEYS, ts), F32), pltpu.VMEM((PEER_NKEYS, ts), F32),
                        pltpu.VMEM((eb, ts), F32), pltpu.VMEM((eb, ts), BF16),
                        pltpu.VMEM((D, ts), F32), pltpu.VMEM((D, ts), BF16), stage, stage],
        compiler_params=_cparams("parallel", "arbitrary"),
        name="peer",
    )(u2, x1, g2, wqT, keys2, ub, vt, ln_g.reshape(1, D), ln_b.reshape(1, D))


def kernel(x, c, positions, w_ada, b_ada, w_in, hy_conv_w, hy_conv_b, hy_f1_w, hy_f1_b, hy_f1_freq, hy_f2_w, hy_f2_b, hy_f2_freq, hy_f3_w, hy_bias, w_branch_hyena, w_branch_attn, w_out, ln1_g, ln1_b, peer_wq, peer_keys, peer_u, peer_v, ln2_g, ln2_b):
    B, S, D = x.shape
    ada = _ada(c, w_ada[0], b_ada[0])
    sh1, sc1, g1, sh2, sc2, g2 = [a.reshape(B, 1, D) for a in jnp.split(ada, 6, axis=-1)]
    phT, qkv0, qkv1, qkv2, gate = _inproj(x, sc1, sh1, positions, w_in[0])
    yhT = _hyena(phT, hy_conv_w[0], hy_conv_b[0], hy_f1_w[0], hy_f1_b[0], hy_f1_freq[0],
                 hy_f2_w[0], hy_f2_b[0], hy_f2_freq[0], hy_f3_w[0], hy_bias[0])
    attn = [_attn_group(v, g) for g, v in enumerate((qkv0, qkv1, qkv2))]
    x1, u2 = _merge(x, yhT, attn, gate, w_branch_hyena[0], w_branch_attn[0], w_out[0],
                    g1, sc2, sh2, ln1_g[0], ln1_b[0])
    out = _peer(u2.reshape(B * S, D), x1.reshape(B * S, D), g2, peer_wq[0], peer_keys[0],
                peer_u[0], peer_v[0], ln2_g[0], ln2_b[0], S)
    return out.reshape(B, S, D)
```

```python
import functools
import math

import numpy as np
import jax
import jax.numpy as jnp
from jax import lax
from jax.experimental import pallas as pl
from jax.experimental.pallas import tpu as pltpu

F32 = jnp.float32
BF16 = jnp.bfloat16

D_MODEL = 1024
HEAD_DIM = 64
HEADS_PER_GROUP = 4
DILATIONS = (1, 4, 16)
ATTN_HALF = 64
N_GROUPS = 3
ATTN_WIDTH = N_GROUPS * HEADS_PER_GROUP * HEAD_DIM
ATTN_OUT = HEADS_PER_GROUP * HEAD_DIM
ROPE_THETA = 500000.0
ROPE_DIM = HEAD_DIM // 4
NEG_BIG = -1e30
HYENA_WIDTH = D_MODEL // 2
HYENA_BANDS = 16
HYENA_HIDDEN = 64
DECAY_SHORT_PCT = 0.3
DECAY_LONG_PCT = 1.5
DECAY_TARGET = 1e-2
PEER_HEADS = 8
PEER_NKEYS = 128
PEER_KEY_DIM = 128
PEER_TOPK = 16
DN_ALPHA = 2.0 ** 0.25
LN_EPS = 1e-5

LANES = 128
VMEM_LIMIT = 56 * 1024 * 1024

HIGHEST = lax.Precision.HIGHEST


def _cparams(*sem, flags=None):
    return pltpu.CompilerParams(dimension_semantics=sem, vmem_limit_bytes=VMEM_LIMIT, flags=flags)


def _ada_kernel(c_ref, w_ref, b_ref, o_ref):
    c = c_ref[...]
    s = c * jax.nn.sigmoid(c)
    o_ref[...] = jnp.dot(s, w_ref[...], precision=HIGHEST, preferred_element_type=F32) + b_ref[...]


def _ada(c, w_ada, b_ada):
    B, D = c.shape
    N = w_ada.shape[1]
    tn = 1024
    return pl.pallas_call(
        _ada_kernel,
        grid=(N // tn,),
        in_specs=[pl.BlockSpec((B, D), lambda j: (0, 0)),
                  pl.BlockSpec((D, tn), lambda j: (0, j)),
                  pl.BlockSpec((1, tn), lambda j: (0, j))],
        out_specs=pl.BlockSpec((B, tn), lambda j: (0, j)),
        out_shape=jax.ShapeDtypeStruct((B, N), F32),
        compiler_params=_cparams("parallel"),
        name="ada",
    )(c, w_ada, b_ada.reshape(1, N))


def _deinterleave_matrix(n, d, transpose=False):
    row = lax.broadcasted_iota(jnp.int32, (n, n), 0)
    col = lax.broadcasted_iota(jnp.int32, (n, n), 1)
    if transpose:
        row, col = col, row
    per = n // d
    return col == (row % per) * d + row // per


def _inproj_kernel(x_ref, sc_ref, sh_ref, pos_ref, inv_ref, whT_ref, wqkv_ref, wg_ref,
                   phT_ref, qkv0_ref, qkv1_ref, qkv2_ref, gate_ref):
    u = x_ref[0] * (1.0 + sc_ref[0]) + sh_ref[0]
    ub = u.astype(BF16)
    tS = ub.shape[0]
    phT_ref[0] = lax.dot_general(whT_ref[...], ub, (((1,), (1,)), ((), ())),
                                 preferred_element_type=F32)
    qkv = jnp.dot(ub, wqkv_ref[...], preferred_element_type=F32)
    ang = pos_ref[0].astype(F32) * inv_ref[...]
    cos_a = jnp.cos(ang)
    sin_a = jnp.sin(ang)
    lane = lax.broadcasted_iota(jnp.int32, (1, LANES), 1) % HEAD_DIM
    half = ROPE_DIM // 2
    s_lo = jnp.where(lane < half, -sin_a, 0.0)
    s_hi = jnp.where((lane >= half) & (lane < ROPE_DIM), sin_a, 0.0)
    n_part = ATTN_WIDTH // LANES
    tiles = []
    for j in range(3 * n_part):
        t = qkv[:, j * LANES:(j + 1) * LANES]
        if j < 2 * n_part:
            t = t * cos_a + pltpu.roll(t, half, 1) * s_hi + pltpu.roll(t, LANES - half, 1) * s_lo
        if j < n_part:
            t = t * (HEAD_DIM ** -0.5)
        tiles.append(t.astype(BF16))
    per_g = ATTN_OUT // LANES
    for g, (d, o_ref) in enumerate(zip(DILATIONS, (qkv0_ref, qkv1_ref, qkv2_ref))):
        xg = jnp.concatenate([tiles[p * n_part + g * per_g + k] for p in range(3) for k in range(per_g)],
                             axis=1)
        if d == 1:
            o_ref[0] = xg
        else:
            perm = jnp.where(_deinterleave_matrix(tS, d), 1.0, 0.0).astype(BF16)
            z = jnp.dot(perm, xg, preferred_element_type=F32).astype(BF16)
            w = xg.shape[1]
            for r in range(d):
                o_ref[0, :, r * w:(r + 1) * w] = z[r * (tS // d):(r + 1) * (tS // d)]
    g = jnp.dot(ub, wg_ref[...], preferred_element_type=F32)
    gate_ref[0] = jax.nn.sigmoid(g).astype(BF16)


def _rope_lane_inv():
    inv = ROPE_THETA ** (-jnp.arange(0, ROPE_DIM, 2, dtype=F32) / ROPE_DIM)
    lane = np.arange(LANES) % HEAD_DIM
    idx = lane % (ROPE_DIM // 2)
    return jnp.where(jnp.asarray(lane < ROPE_DIM), inv[idx], 0.0).reshape(1, LANES).astype(F32)


def _inproj(x, sc1, sh1, positions, w_in):
    B, S, D = x.shape
    hy_cols = 3 * HYENA_WIDTH
    qkv_cols = 3 * ATTN_WIDTH
    whT = w_in[:, :hy_cols].T.astype(BF16)
    wqkv = w_in[:, hy_cols:hy_cols + qkv_cols].astype(BF16)
    wg = w_in[:, hy_cols + qkv_cols:].astype(BF16)
    n_gate = wg.shape[1]
    tS = 256
    gw = 3 * ATTN_OUT
    const = lambda b, i: (0, 0)
    return pl.pallas_call(
        _inproj_kernel,
        grid=(B, S // tS),
        in_specs=[pl.BlockSpec((1, tS, D), lambda b, i: (b, i, 0)),
                  pl.BlockSpec((1, 1, D), lambda b, i: (b, 0, 0)),
                  pl.BlockSpec((1, 1, D), lambda b, i: (b, 0, 0)),
                  pl.BlockSpec((1, tS, 1), lambda b, i: (b, i, 0)),
                  pl.BlockSpec((1, LANES), const),
                  pl.BlockSpec((hy_cols, D), const),
                  pl.BlockSpec((D, qkv_cols), const),
                  pl.BlockSpec((D, n_gate), const)],
        out_specs=[pl.BlockSpec((1, hy_cols, tS), lambda b, i: (b, 0, i))]
                  + [pl.BlockSpec((1, tS // d, d * gw), lambda b, i: (b, i, 0)) for d in DILATIONS]
                  + [pl.BlockSpec((1, tS, n_gate), lambda b, i: (b, i, 0))],
        out_shape=[jax.ShapeDtypeStruct((B, hy_cols, S), F32)]
                  + [jax.ShapeDtypeStruct((B, S // d, d * gw), BF16) for d in DILATIONS]
                  + [jax.ShapeDtypeStruct((B, S, n_gate), BF16)],
        compiler_params=_cparams("parallel", "parallel"),
        name="inproj",
    )(x, sc1, sh1, positions.reshape(B, S, 1), _rope_lane_inv(), whT, wqkv, wg)


ATTN_QSUB = 128
ATTN_KWIN = 256


def _attn_kernel(q_ref, k_ref, v_ref, o_ref, lse_ref, *, n, qb):
    i = pl.program_id(2)
    head_of_lane = lax.broadcasted_iota(jnp.int32, (1, ATTN_OUT), 1) // HEAD_DIM
    for j in range(qb // ATTN_QSUB):
        q0 = i * qb + j * ATTN_QSUB
        ks = pl.multiple_of(jnp.clip(q0 - ATTN_HALF, 0, n - ATTN_KWIN), ATTN_HALF)
        kw = k_ref[0, pl.ds(ks, ATTN_KWIN), :]
        vw = v_ref[0, pl.ds(ks, ATTN_KWIN), :]
        qt = q_ref[0, j * ATTN_QSUB:(j + 1) * ATTN_QSUB, :]
        qi = q0 + lax.broadcasted_iota(jnp.int32, (ATTN_QSUB, 1), 0)
        kj = ks + lax.broadcasted_iota(jnp.int32, (1, ATTN_KWIN), 1)
        valid = jnp.abs(qi - kj) <= ATTN_HALF
        heads = range(HEADS_PER_GROUP)
        hms = [head_of_lane == h for h in heads]
        ss = [lax.dot_general(jnp.where(hm, qt, jnp.zeros_like(qt)), kw, (((1,), (1,)), ((), ())),
                              preferred_element_type=F32) for hm in hms]
        ss = [jnp.where(valid, s, NEG_BIG) for s in ss]
        ms = [jnp.max(s, axis=1, keepdims=True) for s in ss]
        ps = [jnp.exp(s - m) for s, m in zip(ss, ms)]
        ls = [jnp.sum(p, axis=1, keepdims=True) for p in ps]
        os_ = [jnp.dot(p.astype(BF16), vw, preferred_element_type=F32) for p in ps]
        acc = jnp.zeros((ATTN_QSUB, ATTN_OUT), F32)
        lse = jnp.zeros((ATTN_QSUB, ATTN_OUT), F32)
        for hm, o, m, l in zip(hms, os_, ms, ls):
            acc = jnp.where(hm, o * (1.0 / l), acc)
            lse = jnp.where(hm, m + jnp.log(l), lse)
        o_ref[0, j * ATTN_QSUB:(j + 1) * ATTN_QSUB, :] = acc.astype(o_ref.dtype)
        lse_ref[0, j * ATTN_QSUB:(j + 1) * ATTN_QSUB, :] = lse


def _attn_group(view, g):
    B, n, W = view.shape
    d = DILATIONS[g]
    assert W == d * 3 * ATTN_OUT
    qb = min(512, n)
    kern = functools.partial(_attn_kernel, n=n, qb=qb)
    return pl.pallas_call(
        kern,
        grid=(B, d, n // qb),
        in_specs=[pl.BlockSpec((1, qb, ATTN_OUT), lambda b, r, i: (b, i, 3 * r)),
                  pl.BlockSpec((1, n, ATTN_OUT), lambda b, r, i: (b, 0, 3 * r + 1)),
                  pl.BlockSpec((1, n, ATTN_OUT), lambda b, r, i: (b, 0, 3 * r + 2))],
        out_specs=[pl.BlockSpec((1, qb, ATTN_OUT), lambda b, r, i: (b, i, r)),
                   pl.BlockSpec((1, qb, ATTN_OUT), lambda b, r, i: (b, i, r))],
        out_shape=[jax.ShapeDtypeStruct((B, n, d * ATTN_OUT), BF16),
                   jax.ShapeDtypeStruct((B, n, d * ATTN_OUT), F32)],
        compiler_params=_cparams("parallel", "parallel", "arbitrary"),
        name=f"attn_d{d}",
    )(view, view, view)


FFT_R = 128
HYENA_GROUP = 4


def _dft_consts():
    n = np.arange(FFT_R)
    ang = 2.0 * np.pi * np.outer(n, n) / FFT_R
    fr, fi = np.cos(ang), -np.sin(ang)
    tw = 2.0 * np.pi * np.outer(n, n) / (FFT_R * FFT_R)
    twr, twi = np.cos(tw), -np.sin(tw)
    inv_n = 1.0 / (FFT_R * FFT_R)
    half = FFT_R // 2
    c = dict(
        f2=np.concatenate([fr, fi], 0),
        w3=np.block([[fr, fi], [-fi, fr]]),
        w1c=np.block([[fr, -fi], [fi, fr]]),
        c2=np.concatenate([fr[:half] * inv_n, -fi[:half] * inv_n], 0),
    )
    out = {k: jnp.asarray(v, F32).astype(BF16) for k, v in c.items()}
    out["twr"] = jnp.asarray(twr, F32)
    out["twi"] = jnp.asarray(twi, F32)
    return out


def _filt_mlp_kernel(tt_ref, ww_ref, f_ref, w1t_ref, w1c_ref, w1s_ref, b1_ref, fr1_ref,
                     w2_ref, b2_ref, fr2_ref, o_ref):
    t = tt_ref[0]
    fw = f_ref[...] * ww_ref[0]
    pre = (w1t_ref[...] * t
           + jnp.dot(w1c_ref[...], jnp.cos(fw), precision=HIGHEST, preferred_element_type=F32)
           + jnp.dot(w1s_ref[...], -jnp.sin(fw), precision=HIGHEST, preferred_element_type=F32))
    h1 = jnp.sin(fr1_ref[...] * (pre + b1_ref[...]))
    h2 = jnp.dot(w2_ref[...], h1, precision=HIGHEST, preferred_element_type=F32) + b2_ref[...]
    o_ref[0] = jnp.sin(fr2_ref[...] * h2)


def _filt_time_kernel(h2_ref, w3_ref, tt_ref, ad_ref, o_ref):
    half = pl.program_id(2)
    h = jnp.dot(w3_ref[0, 0], h2_ref[0], precision=HIGHEST, preferred_element_type=F32)
    decay = jnp.exp(-tt_ref[0] * ad_ref[...])
    lane = lax.broadcasted_iota(jnp.int32, (1, h.shape[1]), 1)
    o_ref[0] = jnp.where((lane == 0) & (half == 1), 0.0, h * decay)


def _filt_spec_kernel(k_ref, f2_ref, twr_ref, twi_ref, w3_ref, o_ref, *, cf):
    twr = twr_ref[...]
    twi = twi_ref[...]

    for g0 in range(0, cf, HYENA_GROUP):
        cs = range(g0, g0 + HYENA_GROUP)
        kt = jnp.concatenate([k_ref[0, c] for c in cs], axis=1).astype(BF16)
        m = jnp.dot(f2_ref[...], kt, preferred_element_type=F32)
        a_rows = []
        for k in range(HYENA_GROUP):
            ar = m[:FFT_R, k * FFT_R:(k + 1) * FFT_R]
            ai = m[FFT_R:, k * FFT_R:(k + 1) * FFT_R]
            a_rows.append(jnp.concatenate([ar * twr - ai * twi, ar * twi + ai * twr], axis=1))
        xs = jnp.dot(jnp.concatenate(a_rows, axis=0).astype(BF16), w3_ref[...],
                     preferred_element_type=F32)
        for k, c in enumerate(cs):
            o_ref[0, c] = xs[k * FFT_R:(k + 1) * FFT_R]


def _hyena_conv_kernel(cw_ref, cb_ref, hb_ref, v_ref, x1_ref, x2_ref, ks_ref,
                       f2_ref, twr_ref, twi_ref, w3_ref, w1c_ref, c2_ref, o_ref, *, ct, nb):
    ci = pl.program_id(0)
    rows = FFT_R // 2
    twr = twr_ref[...]
    twi = twi_ref[...]
    lane = lax.broadcasted_iota(jnp.int32, (rows, LANES), 1)
    row = lax.broadcasted_iota(jnp.int32, (rows, LANES), 0)
    n_ch = cb_ref.shape[0]
    width = hb_ref.shape[0] // 2

    def short_conv(z, ch):
        r = pltpu.roll(z, 1, 1)
        r2 = pltpu.roll(r, 1, 0)
        prev = jnp.where(lane == 0, jnp.where(row == 0, 0.0, r2), r)
        f = pltpu.roll(z, LANES - 1, 1)
        f2 = pltpu.roll(f, rows - 1, 0)
        nxt = jnp.where(lane == LANES - 1, jnp.where(row == rows - 1, 0.0, f2), f)
        return prev * cw_ref[ch] + z * cw_ref[n_ch + ch] + nxt * cw_ref[2 * n_ch + ch] + cb_ref[ch]

    W2 = 2 * FFT_R

    def long_conv(zs, ks):
        n = len(zs)
        zc = jnp.concatenate([z for pair in zs for z in pair], axis=1).astype(BF16)
        m = jnp.dot(f2_ref[...], zc, preferred_element_type=F32)
        a_rows = []
        for c in range(n):
            mc = m[:, c * W2:(c + 1) * W2]
            ar = mc[:FFT_R, :FFT_R] - mc[FFT_R:, FFT_R:]
            ai = mc[FFT_R:, :FFT_R] + mc[:FFT_R, FFT_R:]
            a_rows.append(jnp.concatenate([ar * twr - ai * twi, ar * twi + ai * twr], axis=1))
        xs = jnp.dot(jnp.concatenate(a_rows, axis=0).astype(BF16), w3_ref[...],
                     preferred_element_type=F32)
        y_rows = []
        for c in range(n):
            xc = xs[c * FFT_R:(c + 1) * FFT_R]
            xr, xi = xc[:, :FFT_R], xc[:, FFT_R:]
            kr, ki = ks[c][:, :FFT_R], ks[c][:, FFT_R:]
            y_rows.append(jnp.concatenate([xr * kr - xi * ki, xr * ki + xi * kr], axis=1))
        bs = jnp.dot(jnp.concatenate(y_rows, axis=0).astype(BF16), w1c_ref[...],
                     preferred_element_type=F32)
        b_cols = []
        for c in range(n):
            bc = bs[c * FFT_R:(c + 1) * FFT_R]
            br, bi = bc[:, :FFT_R], bc[:, FFT_R:]
            b_cols += [br * twr + bi * twi, bi * twr - br * twi]
        m4 = jnp.dot(c2_ref[...], jnp.concatenate(b_cols, axis=1).astype(BF16),
                     preferred_element_type=F32)
        out = []
        for c in range(n):
            mc = m4[:, c * W2:(c + 1) * W2]
            out.append([mc[:rows, :FFT_R] - mc[rows:, FFT_R:], mc[:rows, FFT_R:] + mc[rows:, :FFT_R]])
        return out

    def body(gi, carry):
        cs = [gi * HYENA_GROUP + k for k in range(HYENA_GROUP)]
        z = [[short_conv(v_ref[b, c], ci * ct + c) for b in range(nb)] for c in cs]
        for o, g_ref in enumerate((x1_ref, x2_ref)):
            y = long_conv(z, [ks_ref[o, c] for c in cs])
            z_new = []
            for k, c in enumerate(cs):
                ch = ci * ct + c
                bias = hb_ref[o * width + ch]
                gch = (o + 1) * width + ch
                z_new.append([short_conv(g_ref[b, c], gch) * (y[k][b] + z[k][b] * bias)
                              for b in range(nb)])
            z = z_new
        for k, c in enumerate(cs):
            for b in range(nb):
                o_ref[b, c] = z[k][b].astype(o_ref.dtype)
        return carry

    for gi in range(ct // HYENA_GROUP):
        body(gi, 0)


def _hyena(phT, conv_w, conv_b, f1_w, f1_b, f1_freq, f2_w, f2_b, f2_freq, f3_w, bias):
    B, C3, L = phT.shape
    assert B == 2 and 2 * L == FFT_R * FFT_R
    HW = C3 // 3
    hid = f2_w.shape[0]
    cs = _dft_consts()
    t = jnp.linspace(0.0, 1.0, L, dtype=F32)
    w = 2.0 * math.pi * jnp.arange(L, dtype=F32) / L
    rev = lambda a: jnp.roll(a[::-1], 1)
    tt = jnp.stack([t, rev(t)]).reshape(2, 1, L)
    ww = jnp.stack([w, rev(w)]).reshape(2, 1, L)
    bands = jnp.linspace(1e-4, HYENA_BANDS - 1, HYENA_BANDS, dtype=F32).reshape(HYENA_BANDS, 1)
    max_decay = math.log(DECAY_TARGET) / DECAY_SHORT_PCT
    min_decay = math.log(DECAY_TARGET) / DECAY_LONG_PCT
    absdelta = jnp.abs(jnp.linspace(min_decay, max_decay, HW, dtype=F32)).reshape(HW, 1)

    w1T = f1_w.T.astype(F32)
    col = lambda a: a.astype(F32).reshape(-1, 1)
    full = lambda shp: pl.BlockSpec(shp, lambda *a: (0,) * len(shp))
    h2 = pl.pallas_call(
        _filt_mlp_kernel,
        grid=(2,),
        in_specs=[pl.BlockSpec((1, 1, L), lambda h: (h, 0, 0)),
                  pl.BlockSpec((1, 1, L), lambda h: (h, 0, 0)),
                  full((HYENA_BANDS, 1)), full((hid, 1)), full((hid, HYENA_BANDS)),
                  full((hid, HYENA_BANDS)), full((hid, 1)), full((hid, 1)),
                  full((hid, hid)), full((hid, 1)), full((hid, 1))],
        out_specs=pl.BlockSpec((1, hid, L), lambda h: (h, 0, 0)),
        out_shape=jax.ShapeDtypeStruct((2, hid, L), F32),
        compiler_params=_cparams("parallel"),
        name="hyena_filter_mlp",
    )(tt, ww, bands, w1T[:, :1], w1T[:, 1:1 + HYENA_BANDS], w1T[:, 1 + HYENA_BANDS:],
      col(f1_b), col(f1_freq), f2_w.T.astype(F32), col(f2_b), col(f2_freq))

    w3T = f3_w.T.astype(F32).reshape(2, 2, HW, hid)
    ctf = 128
    kt = pl.pallas_call(
        _filt_time_kernel,
        grid=(2, HW // ctf, 2),
        in_specs=[pl.BlockSpec((1, hid, L), lambda o, c, h: (h, 0, 0)),
                  pl.BlockSpec((1, 1, ctf, hid), lambda o, c, h: (h, o, c, 0)),
                  pl.BlockSpec((1, 1, L), lambda o, c, h: (h, 0, 0)),
                  pl.BlockSpec((ctf, 1), lambda o, c, h: (c, 0))],
        out_specs=pl.BlockSpec((1, ctf, L), lambda o, c, h: (o, c, h)),
        out_shape=jax.ShapeDtypeStruct((2, HW, 2 * L), F32),
        compiler_params=_cparams("parallel", "parallel", "arbitrary"),
        name="hyena_filter_time",
    )(h2, w3T, tt, absdelta)

    cf = 16
    kspec = pl.pallas_call(
        functools.partial(_filt_spec_kernel, cf=cf),
        grid=(2, HW // cf),
        in_specs=[pl.BlockSpec((1, cf, FFT_R, FFT_R), lambda o, c: (o, c, 0, 0)),
                  full((2 * FFT_R, FFT_R)), full((FFT_R, FFT_R)), full((FFT_R, FFT_R)),
                  full((2 * FFT_R, 2 * FFT_R))],
        out_specs=pl.BlockSpec((1, cf, FFT_R, 2 * FFT_R), lambda o, c: (o, c, 0, 0)),
        out_shape=jax.ShapeDtypeStruct((2, HW, FFT_R, 2 * FFT_R), F32),
        compiler_params=_cparams("parallel", "parallel"),
        name="hyena_filter_spec",
    )(kt.reshape(2, HW, FFT_R, FFT_R), cs["f2"], cs["twr"], cs["twi"], cs["w3"])

    ct = 8
    rows = FFT_R // 2
    ph4 = phT.reshape(B, C3, rows, LANES)
    nblk = HW // ct
    smem = pl.BlockSpec(memory_space=pltpu.SMEM)
    yh = pl.pallas_call(
        functools.partial(_hyena_conv_kernel, ct=ct, nb=B),
        grid=(nblk,),
        in_specs=[smem, smem, smem,
                  pl.BlockSpec((B, ct, rows, LANES), lambda c: (0, c, 0, 0)),
                  pl.BlockSpec((B, ct, rows, LANES), lambda c: (0, c + nblk, 0, 0)),
                  pl.BlockSpec((B, ct, rows, LANES), lambda c: (0, c + 2 * nblk, 0, 0)),
                  pl.BlockSpec((2, ct, FFT_R, 2 * FFT_R), lambda c: (0, c, 0, 0)),
                  full((2 * FFT_R, rows)), full((FFT_R, FFT_R)), full((FFT_R, FFT_R)),
                  full((2 * FFT_R, 2 * FFT_R)), full((2 * FFT_R, 2 * FFT_R)), full((FFT_R, FFT_R))],
        out_specs=pl.BlockSpec((B, ct, rows, LANES), lambda c: (0, c, 0, 0)),
        out_shape=jax.ShapeDtypeStruct((B, HW, rows, LANES), BF16),
        compiler_params=_cparams("parallel"),
        name="hyena_conv",
    )(conv_w.astype(F32).reshape(-1), conv_b.astype(F32), bias.astype(F32).reshape(-1),
      ph4, ph4, ph4, kspec, cs["f2"][:, :rows], cs["twr"], cs["twi"], cs["w3"], cs["w1c"], cs["c2"])
    return yh.reshape(B, HW, L)


def _layer_norm(r, g, b):
    mu = jnp.mean(r, axis=-1, keepdims=True)
    d = r - mu
    var = jnp.mean(d * d, axis=-1, keepdims=True)
    return d * lax.rsqrt(var + LN_EPS) * g + b


def _tokens_from_rows(blk, d, is_f32):
    if d == 1:
        return blk
    w = blk.shape[1] // d
    z = jnp.concatenate([blk[:, r * w:(r + 1) * w] for r in range(d)], axis=0)
    perm = jnp.where(_deinterleave_matrix(z.shape[0], d, transpose=True), 1.0, 0.0)
    if is_f32:
        return jnp.dot(perm, z, precision=HIGHEST, preferred_element_type=F32)
    return jnp.dot(perm.astype(BF16), z, preferred_element_type=F32)


def _merge_kernel(x_ref, yh_ref, o0_ref, l0_ref, o1_ref, l1_ref, o2_ref, l2_ref, gate_ref,
                  wbh_ref, wba_ref, wo_ref, g1_ref, sc2_ref, sh2_ref, lng_ref, lnb_ref,
                  x1_ref, u2_ref):
    ls = [_tokens_from_rows(r[0], d, True) for r, d in zip((l0_ref, l1_ref, l2_ref), DILATIONS)]
    os_ = [_tokens_from_rows(r[0], d, False) for r, d in zip((o0_ref, o1_ref, o2_ref), DILATIONS)]
    mx = jnp.maximum(jnp.maximum(ls[0], ls[1]), ls[2])
    ws = [jnp.exp(l - mx) for l in ls]
    den = ws[0] + ws[1] + ws[2]
    ya = (ws[0] * os_[0].astype(F32) + ws[1] * os_[1].astype(F32) + ws[2] * os_[2].astype(F32)) / den
    hb = lax.dot_general(yh_ref[0], wbh_ref[...], (((0,), (0,)), ((), ())),
                         preferred_element_type=F32)
    ab = jnp.dot(ya.astype(BF16), wba_ref[...], preferred_element_type=F32)
    D = hb.shape[1]
    gate = gate_ref[0]
    merged = gate[:, :D].astype(F32) * hb + gate[:, D:].astype(F32) * ab
    mix = jnp.dot(merged.astype(BF16), wo_ref[...], preferred_element_type=F32)
    r = DN_ALPHA * x_ref[0] + (1.0 + g1_ref[0]) * mix
    x1 = _layer_norm(r, lng_ref[...], lnb_ref[...])
    x1_ref[0] = x1
    u2_ref[0] = (x1 * (1.0 + sc2_ref[0]) + sh2_ref[0]).astype(BF16)


def _merge(x, yhT, attn, gate, w_bh, w_ba, w_o, g1, sc2, sh2, ln_g, ln_b):
    B, S, D = x.shape
    HW = yhT.shape[1]
    tS = 256
    tok = lambda w: pl.BlockSpec((1, tS, w), lambda b, i: (b, i, 0))
    vec = pl.BlockSpec((1, 1, D), lambda b, i: (b, 0, 0))
    const = lambda shp: pl.BlockSpec(shp, lambda b, i: (0, 0))
    attn_specs, attn_args = [], []
    for (o, l), d in zip(attn, DILATIONS):
        spec = pl.BlockSpec((1, tS // d, d * ATTN_OUT), lambda b, i: (b, i, 0))
        attn_specs += [spec, spec]
        attn_args += [o, l]
    return pl.pallas_call(
        _merge_kernel,
        grid=(B, S // tS),
        in_specs=[tok(D), pl.BlockSpec((1, HW, tS), lambda b, i: (b, 0, i))] + attn_specs
                 + [tok(2 * D), const((HW, D)), const((ATTN_OUT, D)), const((D, D)),
                    vec, vec, vec, const((1, D)), const((1, D))],
        out_specs=[tok(D), tok(D)],
        out_shape=[jax.ShapeDtypeStruct((B, S, D), F32), jax.ShapeDtypeStruct((B, S, D), BF16)],
        compiler_params=_cparams("parallel", "parallel"),
        name="merge",
    )(x, yhT, *attn_args, gate, w_bh.astype(BF16), w_ba.astype(BF16), w_o.astype(BF16),
      g1, sc2, sh2, ln_g.reshape(1, D), ln_b.reshape(1, D))


PEER_TS = 512
PEER_EB = 1024
PEER_CHUNK = 1024
PEER_BIG = 2.0 ** 100


def _top_values(s, k):
    vals = []
    work = s
    for a in range(k):
        m = jnp.max(work, axis=0, keepdims=True)
        vals.append(m)
        if a + 1 < k:
            work = jnp.where(work == m, -jnp.inf, work)
    return vals


SUBLANES = 8


def _bitonic_desc(v, merge_only=False):
    n = len(v)
    v = list(v)
    k = n if merge_only else 2
    while k <= n:
        j = k // 2
        while j >= 1:
            for i in range(n):
                l = i ^ j
                if l > i:
                    hi, lo = jnp.maximum(v[i], v[l]), jnp.minimum(v[i], v[l])
                    v[i], v[l] = (hi, lo) if (i & k) == 0 else (lo, hi)
            j //= 2
        k *= 2
    return v


def _top_sorted(s, k):
    assert s.shape[0] == k * SUBLANES
    v = _bitonic_desc([s[SUBLANES * r:SUBLANES * (r + 1)] for r in range(k)])
    shift = SUBLANES // 2
    while shift >= 1:
        w = [pltpu.roll(x, shift, 0) for x in v]
        v = _bitonic_desc([jnp.maximum(v[r], w[k - 1 - r]) for r in range(k)], merge_only=True)
        shift //= 2
    return [x[:1] for x in v]


def _gelu(a):
    return 0.5 * a * (1.0 + lax.erf(a * (2.0 ** -0.5)))


def _peer_route(a0, a1):
    K = PEER_TOPK
    p0 = _top_sorted(a0, K)
    p1 = _top_sorted(a1, K)
    sv0_hi = jnp.concatenate(p0[K // 2:], axis=0)
    sv1_lo = jnp.concatenate(p1[:K // 2], axis=0)
    sv1_hi = jnp.concatenate(p1[K // 2:], axis=0)
    cells = [p0[0] + sv1_lo, p0[0] + sv1_hi]
    cells += [p0[a] + sv1_lo for a in range(1, K // 2)]
    cells += [sv0_hi + p1[0]]
    cand = jnp.concatenate(cells, axis=0)
    tv = _top_values(cand, K)
    tau, top = tv[K - 1], tv[0]
    z = jnp.sum(jnp.where(cand >= tau, jnp.exp(cand - top), 0.0), axis=0, keepdims=True)
    in0 = a0 >= p0[K - 1]
    cnt = jnp.zeros_like(a0)
    rank1 = jnp.zeros_like(a1)
    for b in range(K):
        cnt = cnt + jnp.where((a0 + p1[b]) >= tau, 1.0, 0.0)
        rank1 = rank1 + jnp.where(p1[b] > a1, 1.0, 0.0)
    nb = jnp.where(in0, cnt, 0.0)
    c = jnp.where(in0, jnp.exp(a0 - p0[0]) * (1.0 / z), 0.0)
    q = jnp.where(a1 >= p1[K - 1], jnp.exp(a1 - p1[0]), 0.0)
    return c, nb, rank1, q


def _peer_kernel(u2_ref, x1_ref, g2_ref, wq_ref, keys_ref, u_ref, vt_ref, lng_ref, lnb_ref, out_ref,
                 cc_ref, nb_ref, rk_ref, qq_ref, s0_ref, s1_ref, at_ref, cf_ref, acc_ref, u2t_ref,
                 ccs_ref, nbs_ref):
    e = pl.program_id(1)
    ts = u2_ref.shape[0]
    eb = u_ref.shape[0]
    nk = PEER_NKEYS
    n_i = eb // nk
    n_lt = ts // LANES

    @pl.when(e == 0)
    def _prologue():
        acc_ref[...] = jnp.zeros_like(acc_ref)
        u2t_ref[...] = u2_ref[...].astype(F32).T.astype(BF16)

        def head_body(h, carry):
            qt = jnp.dot(wq_ref[h], u2t_ref[...], preferred_element_type=F32)
            s0_ref[...] = jnp.dot(keys_ref[2 * h], qt[:PEER_KEY_DIM].astype(BF16),
                                  preferred_element_type=F32)
            s1_ref[...] = jnp.dot(keys_ref[2 * h + 1], qt[PEER_KEY_DIM:].astype(BF16),
                                  preferred_element_type=F32)
            for lt in range(n_lt):
                sl = slice(lt * LANES, (lt + 1) * LANES)
                c, nb, rank1, q = _peer_route(s0_ref[:, sl], s1_ref[:, sl])
                cc_ref[h, lt] = c
                nb_ref[h, lt] = nb * PEER_BIG
                rk_ref[h, :, sl] = (rank1 * PEER_BIG).astype(BF16)
                qq_ref[h, :, sl] = q.astype(BF16)
            return carry

        lax.fori_loop(0, PEER_HEADS, head_body, 0)

    i0 = pl.multiple_of(e * n_i, n_i)
    for h in range(PEER_HEADS):
        for lt in range(n_lt):
            ccs_ref[h, lt] = cc_ref[h, lt, pl.ds(i0, n_i), :]
            nbs_ref[h, lt] = nb_ref[h, lt, pl.ds(i0, n_i), :]

    per = PEER_CHUNK // nk

    def stage_a(m):
        ms = slice(m * PEER_CHUNK, (m + 1) * PEER_CHUNK)
        at_ref[ms, :] = jnp.dot(u_ref[ms, :], u2t_ref[...], preferred_element_type=F32)

    def stage_b_tiles(i_list, lt):
        sl = slice(lt * LANES, (lt + 1) * LANES)
        gs = [jnp.zeros((nk, LANES), BF16) for _ in i_list]
        for h in range(PEER_HEADS):
            rk = rk_ref[h, :, sl]
            qv = qq_ref[h, :, sl]
            for k, i in enumerate(i_list):
                c_row = ccs_ref[h, lt, i:i + 1, :].astype(BF16)
                nb_row = nbs_ref[h, lt, i:i + 1, :].astype(BF16)
                w = jnp.maximum(nb_row - rk, jnp.zeros((), BF16))
                gs[k] = gs[k] + jnp.minimum(c_row * qv, w)
        for k, i in enumerate(i_list):
            rows = slice(i * nk, (i + 1) * nk)
            cf_ref[rows, sl] = _gelu(at_ref[rows, sl]).astype(BF16) * gs[k]

    stage_a(0)
    for m in range(eb // PEER_CHUNK):
        for i0 in range(m * per, (m + 1) * per, 2):
            for lt in range(n_lt):
                stage_b_tiles([i0, i0 + 1], lt)
                if i0 == m * per and lt == 0 and (m + 1) * PEER_CHUNK < eb:
                    stage_a(m + 1)
        ms = slice(m * PEER_CHUNK, (m + 1) * PEER_CHUNK)
        acc_ref[...] += jnp.dot(vt_ref[:, ms], cf_ref[ms, :], preferred_element_type=F32)

    @pl.when(e == pl.num_programs(1) - 1)
    def _epilogue():
        f = acc_ref[...].T
        r = DN_ALPHA * x1_ref[...] + (1.0 + g2_ref[0]) * f
        out_ref[...] = _layer_norm(r, lng_ref[...], lnb_ref[...])


def _peer(u2, x1, g2, wq, keys, utab, vtab, ln_g, ln_b, seq):
    T, D = u2.shape
    E = utab.shape[0]
    ts, eb = PEER_TS, PEER_EB
    assert eb % PEER_CHUNK == 0 and PEER_CHUNK % PEER_NKEYS == 0
    wqT = wq.T.astype(BF16).reshape(PEER_HEADS, 2 * PEER_KEY_DIM, D)
    keys2 = keys.astype(BF16).reshape(2 * PEER_HEADS, PEER_NKEYS, PEER_KEY_DIM)
    ub = utab.astype(BF16)
    vt = vtab.T.astype(BF16)
    tiles_per_seq = seq // ts
    route = pltpu.VMEM((PEER_HEADS, PEER_NKEYS, ts), BF16)
    rows = pltpu.VMEM((PEER_HEADS, ts // LANES, PEER_NKEYS, LANES), F32)
    stage = pltpu.VMEM((PEER_HEADS, ts // LANES, eb // PEER_NKEYS, LANES), F32)
    return pl.pallas_call(
        _peer_kernel,
        grid=(T // ts, E // eb),
        in_specs=[pl.BlockSpec((ts, D), lambda t, e: (t, 0)),
                  pl.BlockSpec((ts, D), lambda t, e: (t, 0)),
                  pl.BlockSpec((1, 1, D), lambda t, e: (t // tiles_per_seq, 0, 0)),
                  pl.BlockSpec((PEER_HEADS, 2 * PEER_KEY_DIM, D), lambda t, e: (0, 0, 0)),
                  pl.BlockSpec((2 * PEER_HEADS, PEER_NKEYS, PEER_KEY_DIM), lambda t, e: (0, 0, 0)),
                  pl.BlockSpec((eb, D), lambda t, e: (e, 0)),
                  pl.BlockSpec((D, eb), lambda t, e: (0, e)),
                  pl.BlockSpec((1, D), lambda t, e: (0, 0)),
                  pl.BlockSpec((1, D), lambda t, e: (0, 0))],
        out_specs=pl.BlockSpec((ts, D), lambda t, e: (t, 0)),
        out_shape=jax.ShapeDtypeStruct((T, D), F32),
        scratch_shapes=[rows, rows, route, route,
                        pltpu.VMEM((PEER_NKEYS, ts), F32), pltpu.VMEM((PEER_NKEYS, ts), F32),
                        pltpu.VMEM((eb, ts), F32), pltpu.VMEM((eb, ts), BF16),
                        pltpu.VMEM((D, ts), F32), pltpu.VMEM((D, ts), BF16), stage, stage],
        compiler_params=_cparams("parallel", "arbitrary"),
        name="peer",
    )(u2, x1, g2, wqT, keys2, ub, vt, ln_g.reshape(1, D), ln_b.reshape(1, D))


def kernel(x, c, positions, w_ada, b_ada, w_in, hy_conv_w, hy_conv_b, hy_f1_w, hy_f1_b, hy_f1_freq, hy_f2_w, hy_f2_b, hy_f2_freq, hy_f3_w, hy_bias, w_branch_hyena, w_branch_attn, w_out, ln1_g, ln1_b, peer_wq, peer_keys, peer_u, peer_v, ln2_g, ln2_b):
    B, S, D = x.shape
    ada = _ada(c, w_ada[0], b_ada[0])
    sh1, sc1, g1, sh2, sc2, g2 = [a.reshape(B, 1, D) for a in jnp.split(ada, 6, axis=-1)]
    phT, qkv0, qkv1, qkv2, gate = _inproj(x, sc1, sh1, positions, w_in[0])
    yhT = _hyena(phT, hy_conv_w[0], hy_conv_b[0], hy_f1_w[0], hy_f1_b[0], hy_f1_freq[0],
                 hy_f2_w[0], hy_f2_b[0], hy_f2_freq[0], hy_f3_w[0], hy_bias[0])
    attn = [_attn_group(v, g) for g, v in enumerate((qkv0, qkv1, qkv2))]
    x1, u2 = _merge(x, yhT, attn, gate, w_branch_hyena[0], w_branch_attn[0], w_out[0],
                    g1, sc2, sh2, ln1_g[0], ln1_b[0])
    out = _peer(u2.reshape(B * S, D), x1.reshape(B * S, D), g2, peer_wq[0], peer_keys[0],
                peer_u[0], peer_v[0], ln2_g[0], ln2_b[0], S)
    return out.reshape(B, S, D)
```

```python
import functools
import math

import numpy as np
import jax
import jax.numpy as jnp
from jax import lax
from jax.experimental import pallas as pl
from jax.experimental.pallas import tpu as pltpu

F32 = jnp.float32
BF16 = jnp.bfloat16

D_MODEL = 1024
HEAD_DIM = 64
HEADS_PER_GROUP = 4
DILATIONS = (1, 4, 16)
ATTN_HALF = 64
N_GROUPS = 3
ATTN_WIDTH = N_GROUPS * HEADS_PER_GROUP * HEAD_DIM
ATTN_OUT = HEADS_PER_GROUP * HEAD_DIM
ROPE_THETA = 500000.0
ROPE_DIM = HEAD_DIM // 4
NEG_BIG = -1e30
HYENA_WIDTH = D_MODEL // 2
HYENA_BANDS = 16
HYENA_HIDDEN = 64
DECAY_SHORT_PCT = 0.3
DECAY_LONG_PCT = 1.5
DECAY_TARGET = 1e-2
PEER_HEADS = 8
PEER_NKEYS = 128
PEER_KEY_DIM = 128
PEER_TOPK = 16
DN_ALPHA = 2.0 ** 0.25
LN_EPS = 1e-5

LANES = 128
VMEM_LIMIT = 56 * 1024 * 1024

HIGHEST = lax.Precision.HIGHEST


def _cparams(*sem, flags=None):
    return pltpu.CompilerParams(dimension_semantics=sem, vmem_limit_bytes=VMEM_LIMIT, flags=flags)


def _ada_kernel(c_ref, w_ref, b_ref, o_ref):
    c = c_ref[...]
    s = c * jax.nn.sigmoid(c)
    o_ref[...] = jnp.dot(s, w_ref[...], precision=HIGHEST, preferred_element_type=F32) + b_ref[...]


def _ada(c, w_ada, b_ada):
    B, D = c.shape
    N = w_ada.shape[1]
    tn = 1024
    return pl.pallas_call(
        _ada_kernel,
        grid=(N // tn,),
        in_specs=[pl.BlockSpec((B, D), lambda j: (0, 0)),
                  pl.BlockSpec((D, tn), lambda j: (0, j)),
                  pl.BlockSpec((1, tn), lambda j: (0, j))],
        out_specs=pl.BlockSpec((B, tn), lambda j: (0, j)),
        out_shape=jax.ShapeDtypeStruct((B, N), F32),
        compiler_params=_cparams("parallel"),
        name="ada",
    )(c, w_ada, b_ada.reshape(1, N))


def _deinterleave_matrix(n, d, transpose=False):
    row = lax.broadcasted_iota(jnp.int32, (n, n), 0)
    col = lax.broadcasted_iota(jnp.int32, (n, n), 1)
    if transpose:
        row, col = col, row
    per = n // d
    return col == (row % per) * d + row // per


def _inproj_kernel(x_ref, sc_ref, sh_ref, pos_ref, inv_ref, whT_ref, wqkv_ref, wg_ref,
                   phT_ref, qkv0_ref, qkv1_ref, qkv2_ref, gate_ref):
    u = x_ref[0] * (1.0 + sc_ref[0]) + sh_ref[0]
    ub = u.astype(BF16)
    tS = ub.shape[0]
    phT_ref[0] = lax.dot_general(whT_ref[...], ub, (((1,), (1,)), ((), ())),
                                 preferred_element_type=F32)
    qkv = jnp.dot(ub, wqkv_ref[...], preferred_element_type=F32)
    ang = pos_ref[0].astype(F32) * inv_ref[...]
    cos_a = jnp.cos(ang)
    sin_a = jnp.sin(ang)
    lane = lax.broadcasted_iota(jnp.int32, (1, LANES), 1) % HEAD_DIM
    half = ROPE_DIM // 2
    s_lo = jnp.where(lane < half, -sin_a, 0.0)
    s_hi = jnp.where((lane >= half) & (lane < ROPE_DIM), sin_a, 0.0)
    n_part = ATTN_WIDTH // LANES
    tiles = []
    for j in range(3 * n_part):
        t = qkv[:, j * LANES:(j + 1) * LANES]
        if j < 2 * n_part:
            t = t * cos_a + pltpu.roll(t, half, 1) * s_hi + pltpu.roll(t, LANES - half, 1) * s_lo
        if j < n_part:
            t = t * (HEAD_DIM ** -0.5)
        tiles.append(t.astype(BF16))
    per_g = ATTN_OUT // LANES
    for g, (d, o_ref) in enumerate(zip(DILATIONS, (qkv0_ref, qkv1_ref, qkv2_ref))):
        xg = jnp.concatenate([tiles[p * n_part + g * per_g + k] for p in range(3) for k in range(per_g)],
                             axis=1)
        if d == 1:
            o_ref[0] = xg
        else:
            perm = jnp.where(_deinterleave_matrix(tS, d), 1.0, 0.0).astype(BF16)
            z = jnp.dot(perm, xg, preferred_element_type=F32).astype(BF16)
            w = xg.shape[1]
            for r in range(d):
                o_ref[0, :, r * w:(r + 1) * w] = z[r * (tS // d):(r + 1) * (tS // d)]
    g = jnp.dot(ub, wg_ref[...], preferred_element_type=F32)
    gate_ref[0] = jax.nn.sigmoid(g).astype(BF16)


def _rope_lane_inv():
    inv = ROPE_THETA ** (-jnp.arange(0, ROPE_DIM, 2, dtype=F32) / ROPE_DIM)
    lane = np.arange(LANES) % HEAD_DIM
    idx = lane % (ROPE_DIM // 2)
    return jnp.where(jnp.asarray(lane < ROPE_DIM), inv[idx], 0.0).reshape(1, LANES).astype(F32)


def _inproj(x, sc1, sh1, positions, w_in):
    B, S, D = x.shape
    hy_cols = 3 * HYENA_WIDTH
    qkv_cols = 3 * ATTN_WIDTH
    whT = w_in[:, :hy_cols].T.astype(BF16)
    wqkv = w_in[:, hy_cols:hy_cols + qkv_cols].astype(BF16)
    wg = w_in[:, hy_cols + qkv_cols:].astype(BF16)
    n_gate = wg.shape[1]
    tS = 256
    gw = 3 * ATTN_OUT
    const = lambda b, i: (0, 0)
    return pl.pallas_call(
        _inproj_kernel,
        grid=(B, S // tS),
        in_specs=[pl.BlockSpec((1, tS, D), lambda b, i: (b, i, 0)),
                  pl.BlockSpec((1, 1, D), lambda b, i: (b, 0, 0)),
                  pl.BlockSpec((1, 1, D), lambda b, i: (b, 0, 0)),
                  pl.BlockSpec((1, tS, 1), lambda b, i: (b, i, 0)),
                  pl.BlockSpec((1, LANES), const),
                  pl.BlockSpec((hy_cols, D), const),
                  pl.BlockSpec((D, qkv_cols), const),
                  pl.BlockSpec((D, n_gate), const)],
        out_specs=[pl.BlockSpec((1, hy_cols, tS), lambda b, i: (b, 0, i))]
                  + [pl.BlockSpec((1, tS // d, d * gw), lambda b, i: (b, i, 0)) for d in DILATIONS]
                  + [pl.BlockSpec((1, tS, n_gate), lambda b, i: (b, i, 0))],
        out_shape=[jax.ShapeDtypeStruct((B, hy_cols, S), F32)]
                  + [jax.ShapeDtypeStruct((B, S // d, d * gw), BF16) for d in DILATIONS]
                  + [jax.ShapeDtypeStruct((B, S, n_gate), BF16)],
        compiler_params=_cparams("parallel", "parallel"),
        name="inproj",
    )(x, sc1, sh1, positions.reshape(B, S, 1), _rope_lane_inv(), whT, wqkv, wg)


ATTN_QSUB = 128
ATTN_KWIN = 256


def _attn_kernel(q_ref, k_ref, v_ref, o_ref, lse_ref, *, n, qb):
    i = pl.program_id(2)
    head_of_lane = lax.broadcasted_iota(jnp.int32, (1, ATTN_OUT), 1) // HEAD_DIM
    for j in range(qb // ATTN_QSUB):
        q0 = i * qb + j * ATTN_QSUB
        ks = pl.multiple_of(jnp.clip(q0 - ATTN_HALF, 0, n - ATTN_KWIN), ATTN_HALF)
        kw = k_ref[0, pl.ds(ks, ATTN_KWIN), :]
        vw = v_ref[0, pl.ds(ks, ATTN_KWIN), :]
        qt = q_ref[0, j * ATTN_QSUB:(j + 1) * ATTN_QSUB, :]
        qi = q0 + lax.broadcasted_iota(jnp.int32, (ATTN_QSUB, 1), 0)
        kj = ks + lax.broadcasted_iota(jnp.int32, (1, ATTN_KWIN), 1)
        valid = jnp.abs(qi - kj) <= ATTN_HALF
        heads = range(HEADS_PER_GROUP)
        hms = [head_of_lane == h for h in heads]
        ss = [lax.dot_general(jnp.where(hm, qt, jnp.zeros_like(qt)), kw, (((1,), (1,)), ((), ())),
                              preferred_element_type=F32) for hm in hms]
        ss = [jnp.where(valid, s, NEG_BIG) for s in ss]
        ms = [jnp.max(s, axis=1, keepdims=True) for s in ss]
        ps = [jnp.exp(s - m) for s, m in zip(ss, ms)]
        ls = [jnp.sum(p, axis=1, keepdims=True) for p in ps]
        os_ = [jnp.dot(p.astype(BF16), vw, preferred_element_type=F32) for p in ps]
        acc = jnp.zeros((ATTN_QSUB, ATTN_OUT), F32)
        lse = jnp.zeros((ATTN_QSUB, ATTN_OUT), F32)
        for hm, o, m, l in zip(hms, os_, ms, ls):
            acc = jnp.where(hm, o * (1.0 / l), acc)
            lse = jnp.where(hm, m + jnp.log(l), lse)
        o_ref[0, j * ATTN_QSUB:(j + 1) * ATTN_QSUB, :] = acc.astype(o_ref.dtype)
        lse_ref[0, j * ATTN_QSUB:(j + 1) * ATTN_QSUB, :] = lse


def _attn_group(view, g):
    B, n, W = view.shape
    d = DILATIONS[g]
    assert W == d * 3 * ATTN_OUT
    qb = min(512, n)
    kern = functools.partial(_attn_kernel, n=n, qb=qb)
    return pl.pallas_call(
        kern,
        grid=(B, d, n // qb),
        in_specs=[pl.BlockSpec((1, qb, ATTN_OUT), lambda b, r, i: (b, i, 3 * r)),
                  pl.BlockSpec((1, n, ATTN_OUT), lambda b, r, i: (b, 0, 3 * r + 1)),
                  pl.BlockSpec((1, n, ATTN_OUT), lambda b, r, i: (b, 0, 3 * r + 2))],
        out_specs=[pl.BlockSpec((1, qb, ATTN_OUT), lambda b, r, i: (b, i, r)),
                   pl.BlockSpec((1, qb, ATTN_OUT), lambda b, r, i: (b, i, r))],
        out_shape=[jax.ShapeDtypeStruct((B, n, d * ATTN_OUT), BF16),
                   jax.ShapeDtypeStruct((B, n, d * ATTN_OUT), F32)],
        compiler_params=_cparams("parallel", "parallel", "arbitrary"),
        name=f"attn_d{d}",
    )(view, view, view)


FFT_R = 128
HYENA_GROUP = 4


def _dft_consts():
    n = np.arange(FFT_R)
    ang = 2.0 * np.pi * np.outer(n, n) / FFT_R
    fr, fi = np.cos(ang), -np.sin(ang)
    tw = 2.0 * np.pi * np.outer(n, n) / (FFT_R * FFT_R)
    twr, twi = np.cos(tw), -np.sin(tw)
    inv_n = 1.0 / (FFT_R * FFT_R)
    half = FFT_R // 2
    c = dict(
        f2=np.concatenate([fr, fi], 0),
        w3=np.block([[fr, fi], [-fi, fr]]),
        w1c=np.block([[fr, -fi], [fi, fr]]),
        c2=np.concatenate([fr[:half] * inv_n, -fi[:half] * inv_n], 0),
    )
    out = {k: jnp.asarray(v, F32).astype(BF16) for k, v in c.items()}
    out["twr"] = jnp.asarray(twr, F32)
    out["twi"] = jnp.asarray(twi, F32)
    return out


def _filt_mlp_kernel(tt_ref, ww_ref, f_ref, w1t_ref, w1c_ref, w1s_ref, b1_ref, fr1_ref,
                     w2_ref, b2_ref, fr2_ref, o_ref):
    t = tt_ref[0]
    fw = f_ref[...] * ww_ref[0]
    pre = (w1t_ref[...] * t
           + jnp.dot(w1c_ref[...], jnp.cos(fw), precision=HIGHEST, preferred_element_type=F32)
           + jnp.dot(w1s_ref[...], -jnp.sin(fw), precision=HIGHEST, preferred_element_type=F32))
    h1 = jnp.sin(fr1_ref[...] * (pre + b1_ref[...]))
    h2 = jnp.dot(w2_ref[...], h1, precision=HIGHEST, preferred_element_type=F32) + b2_ref[...]
    o_ref[0] = jnp.sin(fr2_ref[...] * h2)


def _filt_time_kernel(h2_ref, w3_ref, tt_ref, ad_ref, o_ref):
    half = pl.program_id(2)
    h = jnp.dot(w3_ref[0, 0], h2_ref[0], precision=HIGHEST, preferred_element_type=F32)
    decay = jnp.exp(-tt_ref[0] * ad_ref[...])
    lane = lax.broadcasted_iota(jnp.int32, (1, h.shape[1]), 1)
    o_ref[0] = jnp.where((lane == 0) & (half == 1), 0.0, h * decay)


def _hyena_conv_kernel(cw_ref, cb_ref, hb_ref, v_ref, x1_ref, x2_ref, kt_ref,
                       f2_ref, f2f_ref, twr_ref, twi_ref, w3_ref, w1c_ref, c2_ref, o_ref, *, ct, nb):
    ci = pl.program_id(0)
    rows = FFT_R // 2
    twr = twr_ref[...]
    twi = twi_ref[...]
    lane = lax.broadcasted_iota(jnp.int32, (rows, LANES), 1)
    row = lax.broadcasted_iota(jnp.int32, (rows, LANES), 0)
    n_ch = cb_ref.shape[0]
    width = hb_ref.shape[0] // 2

    def short_conv(z, ch):
        r = pltpu.roll(z, 1, 1)
        r2 = pltpu.roll(r, 1, 0)
        prev = jnp.where(lane == 0, jnp.where(row == 0, 0.0, r2), r)
        f = pltpu.roll(z, LANES - 1, 1)
        f2 = pltpu.roll(f, rows - 1, 0)
        nxt = jnp.where(lane == LANES - 1, jnp.where(row == rows - 1, 0.0, f2), f)
        return prev * cw_ref[ch] + z * cw_ref[n_ch + ch] + nxt * cw_ref[2 * n_ch + ch] + cb_ref[ch]

    W2 = 2 * FFT_R

    def long_conv(zs, ks):
        n = len(zs)
        zc = jnp.concatenate([z for pair in zs for z in pair], axis=1).astype(BF16)
        m = jnp.dot(f2_ref[...], zc, preferred_element_type=F32)
        a_rows = []
        for c in range(n):
            mc = m[:, c * W2:(c + 1) * W2]
            ar = mc[:FFT_R, :FFT_R] - mc[FFT_R:, FFT_R:]
            ai = mc[FFT_R:, :FFT_R] + mc[:FFT_R, FFT_R:]
            a_rows.append(jnp.concatenate([ar * twr - ai * twi, ar * twi + ai * twr], axis=1))
        xs = jnp.dot(jnp.concatenate(a_rows, axis=0).astype(BF16), w3_ref[...],
                     preferred_element_type=F32)
        y_rows = []
        for c in range(n):
            xc = xs[c * FFT_R:(c + 1) * FFT_R]
            xr, xi = xc[:, :FFT_R], xc[:, FFT_R:]
            kr, ki = ks[c][:, :FFT_R], ks[c][:, FFT_R:]
            y_rows.append(jnp.concatenate([xr * kr - xi * ki, xr * ki + xi * kr], axis=1))
        bs = jnp.dot(jnp.concatenate(y_rows, axis=0).astype(BF16), w1c_ref[...],
                     preferred_element_type=F32)
        b_cols = []
        for c in range(n):
            bc = bs[c * FFT_R:(c + 1) * FFT_R]
            br, bi = bc[:, :FFT_R], bc[:, FFT_R:]
            b_cols += [br * twr + bi * twi, bi * twr - br * twi]
        m4 = jnp.dot(c2_ref[...], jnp.concatenate(b_cols, axis=1).astype(BF16),
                     preferred_element_type=F32)
        out = []
        for c in range(n):
            mc = m4[:, c * W2:(c + 1) * W2]
            out.append([mc[:rows, :FFT_R] - mc[rows:, FFT_R:], mc[:rows, FFT_R:] + mc[rows:, :FFT_R]])
        return out

    def spectra(o, cs):
        kt = jnp.concatenate([kt_ref[o, c] for c in cs], axis=1).astype(BF16)
        m = jnp.dot(f2f_ref[...], kt, preferred_element_type=F32)
        a_rows = []
        for k in range(len(cs)):
            ar = m[:FFT_R, k * FFT_R:(k + 1) * FFT_R]
            ai = m[FFT_R:, k * FFT_R:(k + 1) * FFT_R]
            a_rows.append(jnp.concatenate([ar * twr - ai * twi, ar * twi + ai * twr], axis=1))
        xs = jnp.dot(jnp.concatenate(a_rows, axis=0).astype(BF16), w3_ref[...],
                     preferred_element_type=F32)
        return [xs[k * FFT_R:(k + 1) * FFT_R] for k in range(len(cs))]

    def body(gi, carry):
        cs = [gi * HYENA_GROUP + k for k in range(HYENA_GROUP)]
        z = [[short_conv(v_ref[b, c], ci * ct + c) for b in range(nb)] for c in cs]
        for o, g_ref in enumerate((x1_ref, x2_ref)):
            y = long_conv(z, spectra(o, cs))
            z_new = []
            for k, c in enumerate(cs):
                ch = ci * ct + c
                bias = hb_ref[o * width + ch]
                gch = (o + 1) * width + ch
                z_new.append([short_conv(g_ref[b, c], gch) * (y[k][b] + z[k][b] * bias)
                              for b in range(nb)])
            z = z_new
        for k, c in enumerate(cs):
            for b in range(nb):
                o_ref[b, c] = z[k][b].astype(o_ref.dtype)
        return carry

    for gi in range(ct // HYENA_GROUP):
        body(gi, 0)


def _hyena(phT, conv_w, conv_b, f1_w, f1_b, f1_freq, f2_w, f2_b, f2_freq, f3_w, bias):
    B, C3, L = phT.shape
    assert B == 2 and 2 * L == FFT_R * FFT_R
    HW = C3 // 3
    hid = f2_w.shape[0]
    cs = _dft_consts()
    t = jnp.linspace(0.0, 1.0, L, dtype=F32)
    w = 2.0 * math.pi * jnp.arange(L, dtype=F32) / L
    rev = lambda a: jnp.roll(a[::-1], 1)
    tt = jnp.stack([t, rev(t)]).reshape(2, 1, L)
    ww = jnp.stack([w, rev(w)]).reshape(2, 1, L)
    bands = jnp.linspace(1e-4, HYENA_BANDS - 1, HYENA_BANDS, dtype=F32).reshape(HYENA_BANDS, 1)
    max_decay = math.log(DECAY_TARGET) / DECAY_SHORT_PCT
    min_decay = math.log(DECAY_TARGET) / DECAY_LONG_PCT
    absdelta = jnp.abs(jnp.linspace(min_decay, max_decay, HW, dtype=F32)).reshape(HW, 1)

    w1T = f1_w.T.astype(F32)
    col = lambda a: a.astype(F32).reshape(-1, 1)
    full = lambda shp: pl.BlockSpec(shp, lambda *a: (0,) * len(shp))
    h2 = pl.pallas_call(
        _filt_mlp_kernel,
        grid=(2,),
        in_specs=[pl.BlockSpec((1, 1, L), lambda h: (h, 0, 0)),
                  pl.BlockSpec((1, 1, L), lambda h: (h, 0, 0)),
                  full((HYENA_BANDS, 1)), full((hid, 1)), full((hid, HYENA_BANDS)),
                  full((hid, HYENA_BANDS)), full((hid, 1)), full((hid, 1)),
                  full((hid, hid)), full((hid, 1)), full((hid, 1))],
        out_specs=pl.BlockSpec((1, hid, L), lambda h: (h, 0, 0)),
        out_shape=jax.ShapeDtypeStruct((2, hid, L), F32),
        compiler_params=_cparams("parallel"),
        name="hyena_filter_mlp",
    )(tt, ww, bands, w1T[:, :1], w1T[:, 1:1 + HYENA_BANDS], w1T[:, 1 + HYENA_BANDS:],
      col(f1_b), col(f1_freq), f2_w.T.astype(F32), col(f2_b), col(f2_freq))

    w3T = f3_w.T.astype(F32).reshape(2, 2, HW, hid)
    ctf = 128
    kt = pl.pallas_call(
        _filt_time_kernel,
        grid=(2, HW // ctf, 2),
        in_specs=[pl.BlockSpec((1, hid, L), lambda o, c, h: (h, 0, 0)),
                  pl.BlockSpec((1, 1, ctf, hid), lambda o, c, h: (h, o, c, 0)),
                  pl.BlockSpec((1, 1, L), lambda o, c, h: (h, 0, 0)),
                  pl.BlockSpec((ctf, 1), lambda o, c, h: (c, 0))],
        out_specs=pl.BlockSpec((1, ctf, L), lambda o, c, h: (o, c, h)),
        out_shape=jax.ShapeDtypeStruct((2, HW, 2 * L), F32),
        compiler_params=_cparams("parallel", "parallel", "arbitrary"),
        name="hyena_filter_time",
    )(h2, w3T, tt, absdelta)

    ct = 8
    rows = FFT_R // 2
    ph4 = phT.reshape(B, C3, rows, LANES)
    nblk = HW // ct
    smem = pl.BlockSpec(memory_space=pltpu.SMEM)
    yh = pl.pallas_call(
        functools.partial(_hyena_conv_kernel, ct=ct, nb=B),
        grid=(nblk,),
        in_specs=[smem, smem, smem,
                  pl.BlockSpec((B, ct, rows, LANES), lambda c: (0, c, 0, 0)),
                  pl.BlockSpec((B, ct, rows, LANES), lambda c: (0, c + nblk, 0, 0)),
                  pl.BlockSpec((B, ct, rows, LANES), lambda c: (0, c + 2 * nblk, 0, 0)),
                  pl.BlockSpec((2, ct, FFT_R, FFT_R), lambda c: (0, c, 0, 0)),
                  full((2 * FFT_R, rows)), full((2 * FFT_R, FFT_R)), full((FFT_R, FFT_R)), full((FFT_R, FFT_R)),
                  full((2 * FFT_R, 2 * FFT_R)), full((2 * FFT_R, 2 * FFT_R)), full((FFT_R, FFT_R))],
        out_specs=pl.BlockSpec((B, ct, rows, LANES), lambda c: (0, c, 0, 0)),
        out_shape=jax.ShapeDtypeStruct((B, HW, rows, LANES), BF16),
        compiler_params=_cparams("parallel"),
        name="hyena_conv",
    )(conv_w.astype(F32).reshape(-1), conv_b.astype(F32), bias.astype(F32).reshape(-1),
      ph4, ph4, ph4, kt.reshape(2, HW, FFT_R, FFT_R), cs["f2"][:, :rows], cs["f2"], cs["twr"], cs["twi"],
      cs["w3"], cs["w1c"], cs["c2"])
    return yh.reshape(B, HW, L)


def _layer_norm(r, g, b):
    mu = jnp.mean(r, axis=-1, keepdims=True)
    d = r - mu
    var = jnp.mean(d * d, axis=-1, keepdims=True)
    return d * lax.rsqrt(var + LN_EPS) * g + b


def _tokens_from_rows(blk, d, is_f32):
    if d == 1:
        return blk
    w = blk.shape[1] // d
    z = jnp.concatenate([blk[:, r * w:(r + 1) * w] for r in range(d)], axis=0)
    perm = jnp.where(_deinterleave_matrix(z.shape[0], d, transpose=True), 1.0, 0.0)
    if is_f32:
        return jnp.dot(perm, z, precision=HIGHEST, preferred_element_type=F32)
    return jnp.dot(perm.astype(BF16), z, preferred_element_type=F32)


def _merge_kernel(x_ref, yh_ref, o0_ref, l0_ref, o1_ref, l1_ref, o2_ref, l2_ref, gate_ref,
                  wbh_ref, wba_ref, wo_ref, g1_ref, sc2_ref, sh2_ref, lng_ref, lnb_ref,
                  x1_ref, u2_ref):
    ls = [_tokens_from_rows(r[0], d, True) for r, d in zip((l0_ref, l1_ref, l2_ref), DILATIONS)]
    os_ = [_tokens_from_rows(r[0], d, False) for r, d in zip((o0_ref, o1_ref, o2_ref), DILATIONS)]
    mx = jnp.maximum(jnp.maximum(ls[0], ls[1]), ls[2])
    ws = [jnp.exp(l - mx) for l in ls]
    den = ws[0] + ws[1] + ws[2]
    ya = (ws[0] * os_[0].astype(F32) + ws[1] * os_[1].astype(F32) + ws[2] * os_[2].astype(F32)) / den
    hb = lax.dot_general(yh_ref[0], wbh_ref[...], (((0,), (0,)), ((), ())),
                         preferred_element_type=F32)
    ab = jnp.dot(ya.astype(BF16), wba_ref[...], preferred_element_type=F32)
    D = hb.shape[1]
    gate = gate_ref[0]
    merged = gate[:, :D].astype(F32) * hb + gate[:, D:].astype(F32) * ab
    mix = jnp.dot(merged.astype(BF16), wo_ref[...], preferred_element_type=F32)
    r = DN_ALPHA * x_ref[0] + (1.0 + g1_ref[0]) * mix
    x1 = _layer_norm(r, lng_ref[...], lnb_ref[...])
    x1_ref[0] = x1
    u2_ref[0] = (x1 * (1.0 + sc2_ref[0]) + sh2_ref[0]).astype(BF16)


def _merge(x, yhT, attn, gate, w_bh, w_ba, w_o, g1, sc2, sh2, ln_g, ln_b):
    B, S, D = x.shape
    HW = yhT.shape[1]
    tS = 256
    tok = lambda w: pl.BlockSpec((1, tS, w), lambda b, i: (b, i, 0))
    vec = pl.BlockSpec((1, 1, D), lambda b, i: (b, 0, 0))
    const = lambda shp: pl.BlockSpec(shp, lambda b, i: (0, 0))
    attn_specs, attn_args = [], []
    for (o, l), d in zip(attn, DILATIONS):
        spec = pl.BlockSpec((1, tS // d, d * ATTN_OUT), lambda b, i: (b, i, 0))
        attn_specs += [spec, spec]
        attn_args += [o, l]
    return pl.pallas_call(
        _merge_kernel,
        grid=(B, S // tS),
        in_specs=[tok(D), pl.BlockSpec((1, HW, tS), lambda b, i: (b, 0, i))] + attn_specs
                 + [tok(2 * D), const((HW, D)), const((ATTN_OUT, D)), const((D, D)),
                    vec, vec, vec, const((1, D)), const((1, D))],
        out_specs=[tok(D), tok(D)],
        out_shape=[jax.ShapeDtypeStruct((B, S, D), F32), jax.ShapeDtypeStruct((B, S, D), BF16)],
        compiler_params=_cparams("parallel", "parallel"),
        name="merge",
    )(x, yhT, *attn_args, gate, w_bh.astype(BF16), w_ba.astype(BF16), w_o.astype(BF16),
      g1, sc2, sh2, ln_g.reshape(1, D), ln_b.reshape(1, D))


PEER_TS = 512
PEER_EB = 2048
PEER_CHUNK = 2048
PEER_BIG = 2.0 ** 100


def _top_values(s, k):
    vals = []
    work = s
    for a in range(k):
        m = jnp.max(work, axis=0, keepdims=True)
        vals.append(m)
        if a + 1 < k:
            work = jnp.where(work == m, -jnp.inf, work)
    return vals


SUBLANES = 8


def _bitonic_desc(v, merge_only=False):
    n = len(v)
    v = list(v)
    k = n if merge_only else 2
    while k <= n:
        j = k // 2
        while j >= 1:
            for i in range(n):
                l = i ^ j
                if l > i:
                    hi, lo = jnp.maximum(v[i], v[l]), jnp.minimum(v[i], v[l])
                    v[i], v[l] = (hi, lo) if (i & k) == 0 else (lo, hi)
            j //= 2
        k *= 2
    return v


def _top_sorted(s, k):
    assert s.shape[0] == k * SUBLANES
    v = _bitonic_desc([s[SUBLANES * r:SUBLANES * (r + 1)] for r in range(k)])
    shift = SUBLANES // 2
    while shift >= 1:
        w = [pltpu.roll(x, shift, 0) for x in v]
        v = _bitonic_desc([jnp.maximum(v[r], w[k - 1 - r]) for r in range(k)], merge_only=True)
        shift //= 2
    return [x[:1] for x in v]


def _gelu(a):
    return 0.5 * a * (1.0 + lax.erf(a * (2.0 ** -0.5)))


def _peer_route(a0, a1):
    K = PEER_TOPK
    p0 = _top_sorted(a0, K)
    p1 = _top_sorted(a1, K)
    sv0_hi = jnp.concatenate(p0[K // 2:], axis=0)
    sv1_lo = jnp.concatenate(p1[:K // 2], axis=0)
    sv1_hi = jnp.concatenate(p1[K // 2:], axis=0)
    cells = [p0[0] + sv1_lo, p0[0] + sv1_hi]
    cells += [p0[a] + sv1_lo for a in range(1, K // 2)]
    cells += [sv0_hi + p1[0]]
    cand = jnp.concatenate(cells, axis=0)
    tv = _top_values(cand, K)
    tau, top = tv[K - 1], tv[0]
    z = jnp.sum(jnp.where(cand >= tau, jnp.exp(cand - top), 0.0), axis=0, keepdims=True)
    in0 = a0 >= p0[K - 1]
    cnt = jnp.zeros_like(a0)
    rank1 = jnp.zeros_like(a1)
    for b in range(K):
        cnt = cnt + jnp.where((a0 + p1[b]) >= tau, 1.0, 0.0)
        rank1 = rank1 + jnp.where(p1[b] > a1, 1.0, 0.0)
    nb = jnp.where(in0, cnt, 0.0)
    c = jnp.where(in0, jnp.exp(a0 - p0[0]) * (1.0 / z), 0.0)
    q = jnp.where(a1 >= p1[K - 1], jnp.exp(a1 - p1[0]), 0.0)
    return c, nb, rank1, q


def _peer_kernel(u2_ref, x1_ref, g2_ref, wq_ref, keys_ref, u_ref, vt_ref, lng_ref, lnb_ref, out_ref,
                 cc_ref, nb_ref, rk_ref, qq_ref, s0_ref, s1_ref, at_ref, cf_ref, acc_ref, u2t_ref,
                 ccs_ref, nbs_ref):
    e = pl.program_id(1)
    ts = u2_ref.shape[0]
    eb = u_ref.shape[0]
    nk = PEER_NKEYS
    n_i = eb // nk
    n_lt = ts // LANES

    @pl.when(e == 0)
    def _prologue():
        acc_ref[...] = jnp.zeros_like(acc_ref)
        u2t_ref[...] = u2_ref[...].astype(F32).T.astype(BF16)

        def head_body(h, carry):
            qt = jnp.dot(wq_ref[h], u2t_ref[...], preferred_element_type=F32)
            s0_ref[...] = jnp.dot(keys_ref[2 * h], qt[:PEER_KEY_DIM].astype(BF16),
                                  preferred_element_type=F32)
            s1_ref[...] = jnp.dot(keys_ref[2 * h + 1], qt[PEER_KEY_DIM:].astype(BF16),
                                  preferred_element_type=F32)
            for lt in range(n_lt):
                sl = slice(lt * LANES, (lt + 1) * LANES)
                c, nb, rank1, q = _peer_route(s0_ref[:, sl], s1_ref[:, sl])
                cc_ref[h, lt] = c
                nb_ref[h, lt] = nb * PEER_BIG
                rk_ref[h, :, sl] = (rank1 * PEER_BIG).astype(BF16)
                qq_ref[h, :, sl] = q.astype(BF16)
            return carry

        lax.fori_loop(0, PEER_HEADS, head_body, 0)

    i0 = pl.multiple_of(e * n_i, n_i)
    for h in range(PEER_HEADS):
        for lt in range(n_lt):
            ccs_ref[h, lt] = cc_ref[h, lt, pl.ds(i0, n_i), :]
            nbs_ref[h, lt] = nb_ref[h, lt, pl.ds(i0, n_i), :]

    per = PEER_CHUNK // nk

    def stage_a(m):
        ms = slice(m * PEER_CHUNK, (m + 1) * PEER_CHUNK)
        at_ref[ms, :] = jnp.dot(u_ref[ms, :], u2t_ref[...], preferred_element_type=F32)

    def stage_b_tiles(i_list, lt):
        sl = slice(lt * LANES, (lt + 1) * LANES)
        gs = [jnp.zeros((nk, LANES), BF16) for _ in i_list]
        for h in range(PEER_HEADS):
            rk = rk_ref[h, :, sl]
            qv = qq_ref[h, :, sl]
            for k, i in enumerate(i_list):
                c_row = ccs_ref[h, lt, i:i + 1, :].astype(BF16)
                nb_row = nbs_ref[h, lt, i:i + 1, :].astype(BF16)
                w = jnp.maximum(nb_row - rk, jnp.zeros((), BF16))
                gs[k] = gs[k] + jnp.minimum(c_row * qv, w)
        for k, i in enumerate(i_list):
            rows = slice(i * nk, (i + 1) * nk)
            cf_ref[rows, sl] = _gelu(at_ref[rows, sl]).astype(BF16) * gs[k]

    stage_a(0)
    for m in range(eb // PEER_CHUNK):
        for i0 in range(m * per, (m + 1) * per, 2):
            for lt in range(n_lt):
                stage_b_tiles([i0, i0 + 1], lt)
                if i0 == m * per and lt == 0 and (m + 1) * PEER_CHUNK < eb:
                    stage_a(m + 1)
        ms = slice(m * PEER_CHUNK, (m + 1) * PEER_CHUNK)
        acc_ref[...] += jnp.dot(vt_ref[:, ms], cf_ref[ms, :], preferred_element_type=F32)

    @pl.when(e == pl.num_programs(1) - 1)
    def _epilogue():
        f = acc_ref[...].T
        r = DN_ALPHA * x1_ref[...] + (1.0 + g2_ref[0]) * f
        out_ref[...] = _layer_norm(r, lng_ref[...], lnb_ref[...])


def _peer(u2, x1, g2, wq, keys, utab, vtab, ln_g, ln_b, seq):
    T, D = u2.shape
    E = utab.shape[0]
    ts, eb = PEER_TS, PEER_EB
    assert eb % PEER_CHUNK == 0 and PEER_CHUNK % PEER_NKEYS == 0
    wqT = wq.T.astype(BF16).reshape(PEER_HEADS, 2 * PEER_KEY_DIM, D)
    keys2 = keys.astype(BF16).reshape(2 * PEER_HEADS, PEER_NKEYS, PEER_KEY_DIM)
    ub = utab.astype(BF16)
    vt = vtab.T.astype(BF16)
    tiles_per_seq = seq // ts
    route = pltpu.VMEM((PEER_HEADS, PEER_NKEYS, ts), BF16)
    rows = pltpu.VMEM((PEER_HEADS, ts // LANES, PEER_NKEYS, LANES), F32)
    stage = pltpu.VMEM((PEER_HEADS, ts // LANES, eb // PEER_NKEYS, LANES), F32)
    return pl.pallas_call(
        _peer_kernel,
        grid=(T // ts, E // eb),
        in_specs=[pl.BlockSpec((ts, D), lambda t, e: (t, 0)),
                  pl.BlockSpec((ts, D), lambda t, e: (t, 0)),
                  pl.BlockSpec((1, 1, D), lambda t, e: (t // tiles_per_seq, 0, 0)),
                  pl.BlockSpec((PEER_HEADS, 2 * PEER_KEY_DIM, D), lambda t, e: (0, 0, 0)),
                  pl.BlockSpec((2 * PEER_HEADS, PEER_NKEYS, PEER_KEY_DIM), lambda t, e: (0, 0, 0)),
                  pl.BlockSpec((eb, D), lambda t, e: (e, 0)),
                  pl.BlockSpec((D, eb), lambda t, e: (0, e)),
                  pl.BlockSpec((1, D), lambda t, e: (0, 0)),
                  pl.BlockSpec((1, D), lambda t, e: (0, 0))],
        out_specs=pl.BlockSpec((ts, D), lambda t, e: (t, 0)),
        out_shape=jax.ShapeDtypeStruct((T, D), F32),
        scratch_shapes=[rows, rows, route, route,
                        pltpu.VMEM((PEER_NKEYS, ts), F32), pltpu.VMEM((PEER_NKEYS, ts), F32),
                        pltpu.VMEM((eb, ts), F32), pltpu.VMEM((eb, ts), BF16),
                        pltpu.VMEM((D, ts), F32), pltpu.VMEM((D, ts), BF16), stage, stage],
        compiler_params=_cparams("parallel", "arbitrary"),
        name="peer",
    )(u2, x1, g2, wqT, keys2, ub, vt, ln_g.reshape(1, D), ln_b.reshape(1, D))


def kernel(x, c, positions, w_ada, b_ada, w_in, hy_conv_w, hy_conv_b, hy_f1_w, hy_f1_b, hy_f1_freq, hy_f2_w, hy_f2_b, hy_f2_freq, hy_f3_w, hy_bias, w_branch_hyena, w_branch_attn, w_out, ln1_g, ln1_b, peer_wq, peer_keys, peer_u, peer_v, ln2_g, ln2_b):
    B, S, D = x.shape
    ada = _ada(c, w_ada[0], b_ada[0])
    sh1, sc1, g1, sh2, sc2, g2 = [a.reshape(B, 1, D) for a in jnp.split(ada, 6, axis=-1)]
    phT, qkv0, qkv1, qkv2, gate = _inproj(x, sc1, sh1, positions, w_in[0])
    yhT = _hyena(phT, hy_conv_w[0], hy_conv_b[0], hy_f1_w[0], hy_f1_b[0], hy_f1_freq[0],
                 hy_f2_w[0], hy_f2_b[0], hy_f2_freq[0], hy_f3_w[0], hy_bias[0])
    attn = [_attn_group(v, g) for g, v in enumerate((qkv0, qkv1, qkv2))]
    x1, u2 = _merge(x, yhT, attn, gate, w_branch_hyena[0], w_branch_attn[0], w_out[0],
                    g1, sc2, sh2, ln1_g[0], ln1_b[0])
    out = _peer(u2.reshape(B * S, D), x1.reshape(B * S, D), g2, peer_wq[0], peer_keys[0],
                peer_u[0], peer_v[0], ln2_g[0], ln2_b[0], S)
    return out.reshape(B, S, D)
```

```python
import functools
import math

import numpy as np
import jax
import jax.numpy as jnp
from jax import lax
from jax.experimental import pallas as pl
from jax.experimental.pallas import tpu as pltpu

F32 = jnp.float32
BF16 = jnp.bfloat16

D_MODEL = 1024
HEAD_DIM = 64
HEADS_PER_GROUP = 4
DILATIONS = (1, 4, 16)
ATTN_HALF = 64
N_GROUPS = 3
ATTN_WIDTH = N_GROUPS * HEADS_PER_GROUP * HEAD_DIM
ATTN_OUT = HEADS_PER_GROUP * HEAD_DIM
ROPE_THETA = 500000.0
ROPE_DIM = HEAD_DIM // 4
NEG_BIG = -1e30
HYENA_WIDTH = D_MODEL // 2
HYENA_BANDS = 16
DECAY_SHORT_PCT = 0.3
DECAY_LONG_PCT = 1.5
DECAY_TARGET = 1e-2
PEER_HEADS = 8
PEER_NKEYS = 128
PEER_KEY_DIM = 128
PEER_TOPK = 16
DN_ALPHA = 2.0 ** 0.25
LN_EPS = 1e-5

LANES = 128
VMEM_LIMIT = 56 * 1024 * 1024

HIGHEST = lax.Precision.HIGHEST


def _cparams(*sem, flags=None):
    return pltpu.CompilerParams(dimension_semantics=sem, vmem_limit_bytes=VMEM_LIMIT, flags=flags)


def _ada_kernel(c_ref, w_ref, b_ref, o_ref):
    c = c_ref[...]
    s = c * jax.nn.sigmoid(c)
    o_ref[...] = jnp.dot(s, w_ref[...], precision=HIGHEST, preferred_element_type=F32) + b_ref[...]


def _ada(c, w_ada, b_ada):
    B, D = c.shape
    N = w_ada.shape[1]
    tn = 1024
    return pl.pallas_call(
        _ada_kernel,
        grid=(N // tn,),
        in_specs=[pl.BlockSpec((B, D), lambda j: (0, 0)),
                  pl.BlockSpec((D, tn), lambda j: (0, j)),
                  pl.BlockSpec((1, tn), lambda j: (0, j))],
        out_specs=pl.BlockSpec((B, tn), lambda j: (0, j)),
        out_shape=jax.ShapeDtypeStruct((B, N), F32),
        compiler_params=_cparams("parallel"),
        name="ada",
    )(c, w_ada, b_ada.reshape(1, N))


def _deinterleave_matrix(n, d, transpose=False):
    row = lax.broadcasted_iota(jnp.int32, (n, n), 0)
    col = lax.broadcasted_iota(jnp.int32, (n, n), 1)
    if transpose:
        row, col = col, row
    per = n // d
    return col == (row % per) * d + row // per


def _inproj_kernel(x_ref, sc_ref, sh_ref, pos_ref, inv_ref, whT_ref, wqkv_ref, wg_ref,
                   phT_ref, qkv0_ref, qkv1_ref, qkv2_ref, gate_ref):
    u = x_ref[0] * (1.0 + sc_ref[0]) + sh_ref[0]
    ub = u.astype(BF16)
    tS = ub.shape[0]
    phT_ref[0] = lax.dot_general(whT_ref[...], ub, (((1,), (1,)), ((), ())),
                                 preferred_element_type=F32)
    qkv = jnp.dot(ub, wqkv_ref[...], preferred_element_type=F32)
    ang = pos_ref[0].astype(F32) * inv_ref[...]
    cos_a = jnp.cos(ang)
    sin_a = jnp.sin(ang)
    lane = lax.broadcasted_iota(jnp.int32, (1, LANES), 1) % HEAD_DIM
    half = ROPE_DIM // 2
    s_lo = jnp.where(lane < half, -sin_a, 0.0)
    s_hi = jnp.where((lane >= half) & (lane < ROPE_DIM), sin_a, 0.0)
    n_part = ATTN_WIDTH // LANES
    tiles = []
    for j in range(3 * n_part):
        t = qkv[:, j * LANES:(j + 1) * LANES]
        if j < 2 * n_part:
            t = t * cos_a + pltpu.roll(t, half, 1) * s_hi + pltpu.roll(t, LANES - half, 1) * s_lo
        if j < n_part:
            t = t * (HEAD_DIM ** -0.5)
        tiles.append(t.astype(BF16))
    per_g = ATTN_OUT // LANES
    for g, (d, o_ref) in enumerate(zip(DILATIONS, (qkv0_ref, qkv1_ref, qkv2_ref))):
        xg = jnp.concatenate([tiles[p * n_part + g * per_g + k] for p in range(3) for k in range(per_g)],
                             axis=1)
        if d == 1:
            o_ref[0] = xg
        else:
            perm = jnp.where(_deinterleave_matrix(tS, d), 1.0, 0.0).astype(BF16)
            z = jnp.dot(perm, xg, preferred_element_type=F32).astype(BF16)
            w = xg.shape[1]
            for r in range(d):
                o_ref[0, :, r * w:(r + 1) * w] = z[r * (tS // d):(r + 1) * (tS // d)]
    g = jnp.dot(ub, wg_ref[...], preferred_element_type=F32)
    gate_ref[0] = jax.nn.sigmoid(g).astype(BF16)


def _rope_lane_inv():
    inv = ROPE_THETA ** (-jnp.arange(0, ROPE_DIM, 2, dtype=F32) / ROPE_DIM)
    lane = np.arange(LANES) % HEAD_DIM
    idx = lane % (ROPE_DIM // 2)
    return jnp.where(jnp.asarray(lane < ROPE_DIM), inv[idx], 0.0).reshape(1, LANES).astype(F32)


def _inproj(x, sc1, sh1, positions, w_in):
    B, S, D = x.shape
    hy_cols = 3 * HYENA_WIDTH
    qkv_cols = 3 * ATTN_WIDTH
    whT = w_in[:, :hy_cols].T.astype(BF16)
    wqkv = w_in[:, hy_cols:hy_cols + qkv_cols].astype(BF16)
    wg = w_in[:, hy_cols + qkv_cols:].astype(BF16)
    n_gate = wg.shape[1]
    tS = 256
    gw = 3 * ATTN_OUT
    const = lambda b, i: (0, 0)
    return pl.pallas_call(
        _inproj_kernel,
        grid=(B, S // tS),
        in_specs=[pl.BlockSpec((1, tS, D), lambda b, i: (b, i, 0)),
                  pl.BlockSpec((1, 1, D), lambda b, i: (b, 0, 0)),
                  pl.BlockSpec((1, 1, D), lambda b, i: (b, 0, 0)),
                  pl.BlockSpec((1, tS, 1), lambda b, i: (b, i, 0)),
                  pl.BlockSpec((1, LANES), const),
                  pl.BlockSpec((hy_cols, D), const),
                  pl.BlockSpec((D, qkv_cols), const),
                  pl.BlockSpec((D, n_gate), const)],
        out_specs=[pl.BlockSpec((1, hy_cols, tS), lambda b, i: (b, 0, i))]
                  + [pl.BlockSpec((1, tS // d, d * gw), lambda b, i: (b, i, 0)) for d in DILATIONS]
                  + [pl.BlockSpec((1, tS, n_gate), lambda b, i: (b, i, 0))],
        out_shape=[jax.ShapeDtypeStruct((B, hy_cols, S), F32)]
                  + [jax.ShapeDtypeStruct((B, S // d, d * gw), BF16) for d in DILATIONS]
                  + [jax.ShapeDtypeStruct((B, S, n_gate), BF16)],
        compiler_params=_cparams("parallel", "parallel"),
        name="inproj",
    )(x, sc1, sh1, positions.reshape(B, S, 1), _rope_lane_inv(), whT, wqkv, wg)


ATTN_QSUB = 128
ATTN_KWIN = 256


def _attn_kernel(q_ref, k_ref, v_ref, o_ref, lse_ref, *, n, qb):
    i = pl.program_id(2)
    head_of_lane = lax.broadcasted_iota(jnp.int32, (1, ATTN_OUT), 1) // HEAD_DIM
    for j in range(qb // ATTN_QSUB):
        q0 = i * qb + j * ATTN_QSUB
        ks = pl.multiple_of(jnp.clip(q0 - ATTN_HALF, 0, n - ATTN_KWIN), ATTN_HALF)
        kw = k_ref[0, pl.ds(ks, ATTN_KWIN), :]
        vw = v_ref[0, pl.ds(ks, ATTN_KWIN), :]
        qt = q_ref[0, j * ATTN_QSUB:(j + 1) * ATTN_QSUB, :]
        qi = q0 + lax.broadcasted_iota(jnp.int32, (ATTN_QSUB, 1), 0)
        kj = ks + lax.broadcasted_iota(jnp.int32, (1, ATTN_KWIN), 1)
        valid = jnp.abs(qi - kj) <= ATTN_HALF
        heads = range(HEADS_PER_GROUP)
        hms = [head_of_lane == h for h in heads]
        ss = [lax.dot_general(jnp.where(hm, qt, jnp.zeros_like(qt)), kw, (((1,), (1,)), ((), ())),
                              preferred_element_type=F32) for hm in hms]
        ss = [jnp.where(valid, s, NEG_BIG) for s in ss]
        ms = [jnp.max(s, axis=1, keepdims=True) for s in ss]
        ps = [jnp.exp(s - m) for s, m in zip(ss, ms)]
        ls = [jnp.sum(p, axis=1, keepdims=True) for p in ps]
        os_ = [jnp.dot(p.astype(BF16), vw, preferred_element_type=F32) for p in ps]
        acc = jnp.zeros((ATTN_QSUB, ATTN_OUT), F32)
        lse = jnp.zeros((ATTN_QSUB, ATTN_OUT), F32)
        for hm, o, m, l in zip(hms, os_, ms, ls):
            acc = jnp.where(hm, o * (1.0 / l), acc)
            lse = jnp.where(hm, m + jnp.log(l), lse)
        o_ref[0, j * ATTN_QSUB:(j + 1) * ATTN_QSUB, :] = acc.astype(o_ref.dtype)
        lse_ref[0, j * ATTN_QSUB:(j + 1) * ATTN_QSUB, :] = lse


def _attn_group(view, g):
    B, n, W = view.shape
    d = DILATIONS[g]
    assert W == d * 3 * ATTN_OUT
    qb = min(512, n)
    kern = functools.partial(_attn_kernel, n=n, qb=qb)
    return pl.pallas_call(
        kern,
        grid=(B, d, n // qb),
        in_specs=[pl.BlockSpec((1, qb, ATTN_OUT), lambda b, r, i: (b, i, 3 * r)),
                  pl.BlockSpec((1, n, ATTN_OUT), lambda b, r, i: (b, 0, 3 * r + 1)),
                  pl.BlockSpec((1, n, ATTN_OUT), lambda b, r, i: (b, 0, 3 * r + 2))],
        out_specs=[pl.BlockSpec((1, qb, ATTN_OUT), lambda b, r, i: (b, i, r)),
                   pl.BlockSpec((1, qb, ATTN_OUT), lambda b, r, i: (b, i, r))],
        out_shape=[jax.ShapeDtypeStruct((B, n, d * ATTN_OUT), BF16),
                   jax.ShapeDtypeStruct((B, n, d * ATTN_OUT), F32)],
        compiler_params=_cparams("parallel", "parallel", "arbitrary"),
        name=f"attn_d{d}",
    )(view, view, view)


FFT_R = 128
HYENA_GROUP = 4


def _dft_consts():
    n = np.arange(FFT_R)
    ang = 2.0 * np.pi * np.outer(n, n) / FFT_R
    fr, fi = np.cos(ang), -np.sin(ang)
    tw = 2.0 * np.pi * np.outer(n, n) / (FFT_R * FFT_R)
    twr, twi = np.cos(tw), -np.sin(tw)
    inv_n = 1.0 / (FFT_R * FFT_R)
    half = FFT_R // 2
    c = dict(
        f2=np.concatenate([fr, fi], 0),
        w3=np.block([[fr, fi], [-fi, fr]]),
        w1c=np.block([[fr, -fi], [fi, fr]]),
        c2=np.concatenate([fr[:half] * inv_n, -fi[:half] * inv_n], 0),
    )
    out = {k: jnp.asarray(v, F32).astype(BF16) for k, v in c.items()}
    out["twr"] = jnp.asarray(twr, F32)
    out["twi"] = jnp.asarray(twi, F32)
    return out


def _filt_mlp_kernel(tt_ref, ww_ref, f_ref, w1t_ref, w1c_ref, w1s_ref, b1_ref, fr1_ref,
                     w2_ref, b2_ref, fr2_ref, o_ref):
    t = tt_ref[0]
    fw = f_ref[...] * ww_ref[0]
    pre = (w1t_ref[...] * t
           + jnp.dot(w1c_ref[...], jnp.cos(fw), precision=HIGHEST, preferred_element_type=F32)
           + jnp.dot(w1s_ref[...], -jnp.sin(fw), precision=HIGHEST, preferred_element_type=F32))
    h1 = jnp.sin(fr1_ref[...] * (pre + b1_ref[...]))
    h2 = jnp.dot(w2_ref[...], h1, precision=HIGHEST, preferred_element_type=F32) + b2_ref[...]
    o_ref[0] = jnp.sin(fr2_ref[...] * h2)


def _filt_time_kernel(h2_ref, w3_ref, tt_ref, ad_ref, o_ref):
    half = pl.program_id(2)
    h = jnp.dot(w3_ref[0, 0], h2_ref[0], precision=HIGHEST, preferred_element_type=F32)
    decay = jnp.exp(-tt_ref[0] * ad_ref[...])
    lane = lax.broadcasted_iota(jnp.int32, (1, h.shape[1]), 1)
    o_ref[0] = jnp.where((lane == 0) & (half == 1), 0.0, h * decay)


def _hyena_conv_kernel(cw_ref, cb_ref, hb_ref, v_ref, x1_ref, x2_ref, kt_ref,
                       f2_ref, f2f_ref, twr_ref, twi_ref, w3_ref, w1c_ref, c2_ref, o_ref, *, ct, nb):
    ci = pl.program_id(0)
    rows = FFT_R // 2
    twr = twr_ref[...]
    twi = twi_ref[...]
    lane = lax.broadcasted_iota(jnp.int32, (rows, LANES), 1)
    row = lax.broadcasted_iota(jnp.int32, (rows, LANES), 0)
    n_ch = cb_ref.shape[0]
    width = hb_ref.shape[0] // 2

    def short_conv(z, ch):
        r = pltpu.roll(z, 1, 1)
        r2 = pltpu.roll(r, 1, 0)
        prev = jnp.where(lane == 0, jnp.where(row == 0, 0.0, r2), r)
        f = pltpu.roll(z, LANES - 1, 1)
        f2 = pltpu.roll(f, rows - 1, 0)
        nxt = jnp.where(lane == LANES - 1, jnp.where(row == rows - 1, 0.0, f2), f)
        return prev * cw_ref[ch] + z * cw_ref[n_ch + ch] + nxt * cw_ref[2 * n_ch + ch] + cb_ref[ch]

    W2 = 2 * FFT_R

    def long_conv(zs, ks):
        n = len(zs)
        zc = jnp.concatenate([z for pair in zs for z in pair], axis=1).astype(BF16)
        m = jnp.dot(f2_ref[...], zc, preferred_element_type=F32)
        a_rows = []
        for c in range(n):
            mc = m[:, c * W2:(c + 1) * W2]
            ar = mc[:FFT_R, :FFT_R] - mc[FFT_R:, FFT_R:]
            ai = mc[FFT_R:, :FFT_R] + mc[:FFT_R, FFT_R:]
            a_rows.append(jnp.concatenate([ar * twr - ai * twi, ar * twi + ai * twr], axis=1))
        xs = jnp.dot(jnp.concatenate(a_rows, axis=0).astype(BF16), w3_ref[...],
                     preferred_element_type=F32)
        y_rows = []
        for c in range(n):
            xc = xs[c * FFT_R:(c + 1) * FFT_R]
            xr, xi = xc[:, :FFT_R], xc[:, FFT_R:]
            kr, ki = ks[c][:, :FFT_R], ks[c][:, FFT_R:]
            y_rows.append(jnp.concatenate([xr * kr - xi * ki, xr * ki + xi * kr], axis=1))
        bs = jnp.dot(jnp.concatenate(y_rows, axis=0).astype(BF16), w1c_ref[...],
                     preferred_element_type=F32)
        b_cols = []
        for c in range(n):
            bc = bs[c * FFT_R:(c + 1) * FFT_R]
            br, bi = bc[:, :FFT_R], bc[:, FFT_R:]
            b_cols += [br * twr + bi * twi, bi * twr - br * twi]
        m4 = jnp.dot(c2_ref[...], jnp.concatenate(b_cols, axis=1).astype(BF16),
                     preferred_element_type=F32)
        out = []
        for c in range(n):
            mc = m4[:, c * W2:(c + 1) * W2]
            out.append([mc[:rows, :FFT_R] - mc[rows:, FFT_R:], mc[:rows, FFT_R:] + mc[rows:, :FFT_R]])
        return out

    def spectra(o, cs):
        kt = jnp.concatenate([kt_ref[o, c] for c in cs], axis=1).astype(BF16)
        m = jnp.dot(f2f_ref[...], kt, preferred_element_type=F32)
        a_rows = []
        for k in range(len(cs)):
            ar = m[:FFT_R, k * FFT_R:(k + 1) * FFT_R]
            ai = m[FFT_R:, k * FFT_R:(k + 1) * FFT_R]
            a_rows.append(jnp.concatenate([ar * twr - ai * twi, ar * twi + ai * twr], axis=1))
        xs = jnp.dot(jnp.concatenate(a_rows, axis=0).astype(BF16), w3_ref[...],
                     preferred_element_type=F32)
        return [xs[k * FFT_R:(k + 1) * FFT_R] for k in range(len(cs))]

    def body(gi, carry):
        cs = [gi * HYENA_GROUP + k for k in range(HYENA_GROUP)]
        z = [[short_conv(v_ref[b, c], ci * ct + c) for b in range(nb)] for c in cs]
        for o, g_ref in enumerate((x1_ref, x2_ref)):
            y = long_conv(z, spectra(o, cs))
            z_new = []
            for k, c in enumerate(cs):
                ch = ci * ct + c
                bias = hb_ref[o * width + ch]
                gch = (o + 1) * width + ch
                z_new.append([short_conv(g_ref[b, c], gch) * (y[k][b] + z[k][b] * bias)
                              for b in range(nb)])
            z = z_new
        for k, c in enumerate(cs):
            for b in range(nb):
                o_ref[b, c] = z[k][b].astype(o_ref.dtype)
        return carry

    for gi in range(ct // HYENA_GROUP):
        body(gi, 0)


def _hyena(phT, conv_w, conv_b, f1_w, f1_b, f1_freq, f2_w, f2_b, f2_freq, f3_w, bias):
    B, C3, L = phT.shape
    assert B == 2 and 2 * L == FFT_R * FFT_R
    HW = C3 // 3
    hid = f2_w.shape[0]
    cs = _dft_consts()
    t = jnp.linspace(0.0, 1.0, L, dtype=F32)
    w = 2.0 * math.pi * jnp.arange(L, dtype=F32) / L
    rev = lambda a: jnp.roll(a[::-1], 1)
    tt = jnp.stack([t, rev(t)]).reshape(2, 1, L)
    ww = jnp.stack([w, rev(w)]).reshape(2, 1, L)
    bands = jnp.linspace(1e-4, HYENA_BANDS - 1, HYENA_BANDS, dtype=F32).reshape(HYENA_BANDS, 1)
    max_decay = math.log(DECAY_TARGET) / DECAY_SHORT_PCT
    min_decay = math.log(DECAY_TARGET) / DECAY_LONG_PCT
    absdelta = jnp.abs(jnp.linspace(min_decay, max_decay, HW, dtype=F32)).reshape(HW, 1)

    w1T = f1_w.T.astype(F32)
    col = lambda a: a.astype(F32).reshape(-1, 1)
    full = lambda shp: pl.BlockSpec(shp, lambda *a: (0,) * len(shp))
    h2 = pl.pallas_call(
        _filt_mlp_kernel,
        grid=(2,),
        in_specs=[pl.BlockSpec((1, 1, L), lambda h: (h, 0, 0)),
                  pl.BlockSpec((1, 1, L), lambda h: (h, 0, 0)),
                  full((HYENA_BANDS, 1)), full((hid, 1)), full((hid, HYENA_BANDS)),
                  full((hid, HYENA_BANDS)), full((hid, 1)), full((hid, 1)),
                  full((hid, hid)), full((hid, 1)), full((hid, 1))],
        out_specs=pl.BlockSpec((1, hid, L), lambda h: (h, 0, 0)),
        out_shape=jax.ShapeDtypeStruct((2, hid, L), F32),
        compiler_params=_cparams("parallel"),
        name="hyena_filter_mlp",
    )(tt, ww, bands, w1T[:, :1], w1T[:, 1:1 + HYENA_BANDS], w1T[:, 1 + HYENA_BANDS:],
      col(f1_b), col(f1_freq), f2_w.T.astype(F32), col(f2_b), col(f2_freq))

    w3T = f3_w.T.astype(F32).reshape(2, 2, HW, hid)
    ctf = 128
    kt = pl.pallas_call(
        _filt_time_kernel,
        grid=(2, HW // ctf, 2),
        in_specs=[pl.BlockSpec((1, hid, L), lambda o, c, h: (h, 0, 0)),
                  pl.BlockSpec((1, 1, ctf, hid), lambda o, c, h: (h, o, c, 0)),
                  pl.BlockSpec((1, 1, L), lambda o, c, h: (h, 0, 0)),
                  pl.BlockSpec((ctf, 1), lambda o, c, h: (c, 0))],
        out_specs=pl.BlockSpec((1, ctf, L), lambda o, c, h: (o, c, h)),
        out_shape=jax.ShapeDtypeStruct((2, HW, 2 * L), F32),
        compiler_params=_cparams("parallel", "parallel", "arbitrary"),
        name="hyena_filter_time",
    )(h2, w3T, tt, absdelta)

    ct = 8
    rows = FFT_R // 2
    ph4 = phT.reshape(B, C3, rows, LANES)
    nblk = HW // ct
    smem = pl.BlockSpec(memory_space=pltpu.SMEM)
    yh = pl.pallas_call(
        functools.partial(_hyena_conv_kernel, ct=ct, nb=B),
        grid=(nblk,),
        in_specs=[smem, smem, smem,
                  pl.BlockSpec((B, ct, rows, LANES), lambda c: (0, c, 0, 0)),
                  pl.BlockSpec((B, ct, rows, LANES), lambda c: (0, c + nblk, 0, 0)),
                  pl.BlockSpec((B, ct, rows, LANES), lambda c: (0, c + 2 * nblk, 0, 0)),
                  pl.BlockSpec((2, ct, FFT_R, FFT_R), lambda c: (0, c, 0, 0)),
                  full((2 * FFT_R, rows)), full((2 * FFT_R, FFT_R)), full((FFT_R, FFT_R)), full((FFT_R, FFT_R)),
                  full((2 * FFT_R, 2 * FFT_R)), full((2 * FFT_R, 2 * FFT_R)), full((FFT_R, FFT_R))],
        out_specs=pl.BlockSpec((B, ct, rows, LANES), lambda c: (0, c, 0, 0)),
        out_shape=jax.ShapeDtypeStruct((B, HW, rows, LANES), BF16),
        compiler_params=_cparams("parallel"),
        name="hyena_conv",
    )(conv_w.astype(F32).reshape(-1), conv_b.astype(F32), bias.astype(F32).reshape(-1),
      ph4, ph4, ph4, kt.reshape(2, HW, FFT_R, FFT_R), cs["f2"][:, :rows], cs["f2"], cs["twr"], cs["twi"],
      cs["w3"], cs["w1c"], cs["c2"])
    return yh.reshape(B, HW, L)


def _layer_norm(r, g, b):
    mu = jnp.mean(r, axis=-1, keepdims=True)
    d = r - mu
    var = jnp.mean(d * d, axis=-1, keepdims=True)
    return d * lax.rsqrt(var + LN_EPS) * g + b


def _tokens_from_rows(blk, d, is_f32):
    if d == 1:
        return blk
    w = blk.shape[1] // d
    z = jnp.concatenate([blk[:, r * w:(r + 1) * w] for r in range(d)], axis=0)
    perm = jnp.where(_deinterleave_matrix(z.shape[0], d, transpose=True), 1.0, 0.0)
    if is_f32:
        return jnp.dot(perm, z, precision=HIGHEST, preferred_element_type=F32)
    return jnp.dot(perm.astype(BF16), z, preferred_element_type=F32)


def _merge_kernel(x_ref, yh_ref, o0_ref, l0_ref, o1_ref, l1_ref, o2_ref, l2_ref, gate_ref,
                  wbh_ref, wba_ref, wo_ref, g1_ref, sc2_ref, sh2_ref, lng_ref, lnb_ref,
                  x1_ref, u2_ref):
    ls = [_tokens_from_rows(r[0], d, True) for r, d in zip((l0_ref, l1_ref, l2_ref), DILATIONS)]
    os_ = [_tokens_from_rows(r[0], d, False) for r, d in zip((o0_ref, o1_ref, o2_ref), DILATIONS)]
    mx = jnp.maximum(jnp.maximum(ls[0], ls[1]), ls[2])
    ws = [jnp.exp(l - mx) for l in ls]
    den = ws[0] + ws[1] + ws[2]
    ya = (ws[0] * os_[0].astype(F32) + ws[1] * os_[1].astype(F32) + ws[2] * os_[2].astype(F32)) / den
    hb = lax.dot_general(yh_ref[0], wbh_ref[...], (((0,), (0,)), ((), ())),
                         preferred_element_type=F32)
    ab = jnp.dot(ya.astype(BF16), wba_ref[...], preferred_element_type=F32)
    D = hb.shape[1]
    gate = gate_ref[0]
    merged = gate[:, :D].astype(F32) * hb + gate[:, D:].astype(F32) * ab
    mix = jnp.dot(merged.astype(BF16), wo_ref[...], preferred_element_type=F32)
    r = DN_ALPHA * x_ref[0] + (1.0 + g1_ref[0]) * mix
    x1 = _layer_norm(r, lng_ref[...], lnb_ref[...])
    x1_ref[0] = x1
    u2_ref[0] = (x1 * (1.0 + sc2_ref[0]) + sh2_ref[0]).astype(BF16)


def _merge(x, yhT, attn, gate, w_bh, w_ba, w_o, g1, sc2, sh2, ln_g, ln_b):
    B, S, D = x.shape
    HW = yhT.shape[1]
    tS = 256
    tok = lambda w: pl.BlockSpec((1, tS, w), lambda b, i: (b, i, 0))
    vec = pl.BlockSpec((1, 1, D), lambda b, i: (b, 0, 0))
    const = lambda shp: pl.BlockSpec(shp, lambda b, i: (0, 0))
    attn_specs, attn_args = [], []
    for (o, l), d in zip(attn, DILATIONS):
        spec = pl.BlockSpec((1, tS // d, d * ATTN_OUT), lambda b, i: (b, i, 0))
        attn_specs += [spec, spec]
        attn_args += [o, l]
    return pl.pallas_call(
        _merge_kernel,
        grid=(B, S // tS),
        in_specs=[tok(D), pl.BlockSpec((1, HW, tS), lambda b, i: (b, 0, i))] + attn_specs
                 + [tok(2 * D), const((HW, D)), const((ATTN_OUT, D)), const((D, D)),
                    vec, vec, vec, const((1, D)), const((1, D))],
        out_specs=[tok(D), tok(D)],
        out_shape=[jax.ShapeDtypeStruct((B, S, D), F32), jax.ShapeDtypeStruct((B, S, D), BF16)],
        compiler_params=_cparams("parallel", "parallel"),
        name="merge",
    )(x, yhT, *attn_args, gate, w_bh.astype(BF16), w_ba.astype(BF16), w_o.astype(BF16),
      g1, sc2, sh2, ln_g.reshape(1, D), ln_b.reshape(1, D))


PEER_TS = 512
PEER_EB = 2048
PEER_BIG = 2.0 ** 100


def _top_values(s, k):
    vals = []
    work = s
    for a in range(k):
        m = jnp.max(work, axis=0, keepdims=True)
        vals.append(m)
        if a + 1 < k:
            work = jnp.where(work == m, -jnp.inf, work)
    return vals


SUBLANES = 8


def _bitonic_desc(v, merge_only=False):
    n = len(v)
    v = list(v)
    k = n if merge_only else 2
    while k <= n:
        j = k // 2
        while j >= 1:
            for i in range(n):
                l = i ^ j
                if l > i:
                    hi, lo = jnp.maximum(v[i], v[l]), jnp.minimum(v[i], v[l])
                    v[i], v[l] = (hi, lo) if (i & k) == 0 else (lo, hi)
            j //= 2
        k *= 2
    return v


def _top_sorted(s, k):
    assert s.shape[0] == k * SUBLANES
    v = _bitonic_desc([s[SUBLANES * r:SUBLANES * (r + 1)] for r in range(k)])
    shift = SUBLANES // 2
    while shift >= 1:
        w = [pltpu.roll(x, shift, 0) for x in v]
        v = _bitonic_desc([jnp.maximum(v[r], w[k - 1 - r]) for r in range(k)], merge_only=True)
        shift //= 2
    return [x[:1] for x in v]


def _gelu(a):
    return 0.5 * a * (1.0 + lax.erf(a * (2.0 ** -0.5)))


def _peer_route(a0, a1):
    K = PEER_TOPK
    p0 = _top_sorted(a0, K)
    p1 = _top_sorted(a1, K)
    sv0_hi = jnp.concatenate(p0[K // 2:], axis=0)
    sv1_lo = jnp.concatenate(p1[:K // 2], axis=0)
    sv1_hi = jnp.concatenate(p1[K // 2:], axis=0)
    cells = [p0[0] + sv1_lo, p0[0] + sv1_hi]
    cells += [p0[a] + sv1_lo for a in range(1, K // 2)]
    cells += [sv0_hi + p1[0]]
    cand = jnp.concatenate(cells, axis=0)
    tv = _top_values(cand, K)
    tau, top = tv[K - 1], tv[0]
    z = jnp.sum(jnp.where(cand >= tau, jnp.exp(cand - top), 0.0), axis=0, keepdims=True)
    in0 = a0 >= p0[K - 1]
    cnt = jnp.zeros_like(a0)
    rank1 = jnp.zeros_like(a1)
    for b in range(K):
        cnt = cnt + jnp.where((a0 + p1[b]) >= tau, 1.0, 0.0)
        rank1 = rank1 + jnp.where(p1[b] > a1, 1.0, 0.0)
    nb = jnp.where(in0, cnt, 0.0)
    c = jnp.where(in0, jnp.exp(a0 - p0[0]) * (1.0 / z), 0.0)
    q = jnp.where(a1 >= p1[K - 1], jnp.exp(a1 - p1[0]), 0.0)
    return c, nb, rank1, q


def _peer_kernel(u2_ref, x1_ref, g2_ref, wq_ref, keys_ref, u_ref, vt_ref, lng_ref, lnb_ref, out_ref,
                 cc_ref, nb_ref, rk_ref, qq_ref, s0_ref, s1_ref, at_ref, cf_ref, acc_ref, u2t_ref,
                 ccs_ref, nbs_ref):
    e = pl.program_id(1)
    ts = u2_ref.shape[0]
    eb = u_ref.shape[0]
    nk = PEER_NKEYS
    n_i = eb // nk
    n_lt = ts // LANES

    @pl.when(e == 0)
    def _prologue():
        acc_ref[...] = jnp.zeros_like(acc_ref)
        u2t_ref[...] = u2_ref[...].astype(F32).T.astype(BF16)

        def head_body(h, carry):
            qt = jnp.dot(wq_ref[h], u2t_ref[...], preferred_element_type=F32)
            s0_ref[...] = jnp.dot(keys_ref[2 * h], qt[:PEER_KEY_DIM].astype(BF16),
                                  preferred_element_type=F32)
            s1_ref[...] = jnp.dot(keys_ref[2 * h + 1], qt[PEER_KEY_DIM:].astype(BF16),
                                  preferred_element_type=F32)
            for lt in range(n_lt):
                sl = slice(lt * LANES, (lt + 1) * LANES)
                c, nb, rank1, q = _peer_route(s0_ref[:, sl], s1_ref[:, sl])
                cc_ref[h, lt] = c
                nb_ref[h, lt] = nb * PEER_BIG
                rk_ref[h, :, sl] = (rank1 * PEER_BIG).astype(BF16)
                qq_ref[h, :, sl] = q.astype(BF16)
            return carry

        lax.fori_loop(0, PEER_HEADS, head_body, 0)

    base = pl.multiple_of(e * n_i, n_i)
    for h in range(PEER_HEADS):
        for lt in range(n_lt):
            ccs_ref[h, lt] = cc_ref[h, lt, pl.ds(base, n_i), :]
            nbs_ref[h, lt] = nb_ref[h, lt, pl.ds(base, n_i), :]

    def gate_tiles(i_list, lt):
        sl = slice(lt * LANES, (lt + 1) * LANES)
        gs = [jnp.zeros((nk, LANES), BF16) for _ in i_list]
        for h in range(PEER_HEADS):
            rk = rk_ref[h, :, sl]
            qv = qq_ref[h, :, sl]
            for k, i in enumerate(i_list):
                c_row = ccs_ref[h, lt, i:i + 1, :].astype(BF16)
                nb_row = nbs_ref[h, lt, i:i + 1, :].astype(BF16)
                w = jnp.maximum(nb_row - rk, jnp.zeros((), BF16))
                gs[k] = gs[k] + jnp.minimum(c_row * qv, w)
        for k, i in enumerate(i_list):
            rows = slice(i * nk, (i + 1) * nk)
            cf_ref[rows, sl] = _gelu(at_ref[rows, sl]).astype(BF16) * gs[k]

    at_ref[...] = jnp.dot(u_ref[...], u2t_ref[...], preferred_element_type=F32)
    for i in range(0, n_i, 2):
        for lt in range(n_lt):
            gate_tiles([i, i + 1], lt)
    acc_ref[...] += jnp.dot(vt_ref[...], cf_ref[...], preferred_element_type=F32)

    @pl.when(e == pl.num_programs(1) - 1)
    def _epilogue():
        f = acc_ref[...].T
        r = DN_ALPHA * x1_ref[...] + (1.0 + g2_ref[0]) * f
        out_ref[...] = _layer_norm(r, lng_ref[...], lnb_ref[...])


def _peer(u2, x1, g2, wq, keys, utab, vtab, ln_g, ln_b, seq):
    T, D = u2.shape
    E = utab.shape[0]
    ts, eb = PEER_TS, PEER_EB
    assert E % eb == 0 and eb % (2 * PEER_NKEYS) == 0 and T % ts == 0
    wqT = wq.T.astype(BF16).reshape(PEER_HEADS, 2 * PEER_KEY_DIM, D)
    keys2 = keys.astype(BF16).reshape(2 * PEER_HEADS, PEER_NKEYS, PEER_KEY_DIM)
    ub = utab.astype(BF16)
    vt = vtab.T.astype(BF16)
    tiles_per_seq = seq // ts
    route = pltpu.VMEM((PEER_HEADS, PEER_NKEYS, ts), BF16)
    rows = pltpu.VMEM((PEER_HEADS, ts // LANES, PEER_NKEYS, LANES), F32)
    stage = pltpu.VMEM((PEER_HEADS, ts // LANES, eb // PEER_NKEYS, LANES), F32)
    return pl.pallas_call(
        _peer_kernel,
        grid=(T // ts, E // eb),
        in_specs=[pl.BlockSpec((ts, D), lambda t, e: (t, 0)),
                  pl.BlockSpec((ts, D), lambda t, e: (t, 0)),
                  pl.BlockSpec((1, 1, D), lambda t, e: (t // tiles_per_seq, 0, 0)),
                  pl.BlockSpec((PEER_HEADS, 2 * PEER_KEY_DIM, D), lambda t, e: (0, 0, 0)),
                  pl.BlockSpec((2 * PEER_HEADS, PEER_NKEYS, PEER_KEY_DIM), lambda t, e: (0, 0, 0)),
                  pl.BlockSpec((eb, D), lambda t, e: (e, 0)),
                  pl.BlockSpec((D, eb), lambda t, e: (0, e)),
                  pl.BlockSpec((1, D), lambda t, e: (0, 0)),
                  pl.BlockSpec((1, D), lambda t, e: (0, 0))],
        out_specs=pl.BlockSpec((ts, D), lambda t, e: (t, 0)),
        out_shape=jax.ShapeDtypeStruct((T, D), F32),
        scratch_shapes=[rows, rows, route, route,
                        pltpu.VMEM((PEER_NKEYS, ts), F32), pltpu.VMEM((PEER_NKEYS, ts), F32),
                        pltpu.VMEM((eb, ts), F32), pltpu.VMEM((eb, ts), BF16),
                        pltpu.VMEM((D, ts), F32), pltpu.VMEM((D, ts), BF16), stage, stage],
        compiler_params=_cparams("parallel", "arbitrary"),
        name="peer",
    )(u2, x1, g2, wqT, keys2, ub, vt, ln_g.reshape(1, D), ln_b.reshape(1, D))


def kernel(x, c, positions, w_ada, b_ada, w_in, hy_conv_w, hy_conv_b, hy_f1_w, hy_f1_b, hy_f1_freq, hy_f2_w, hy_f2_b, hy_f2_freq, hy_f3_w, hy_bias, w_branch_hyena, w_branch_attn, w_out, ln1_g, ln1_b, peer_wq, peer_keys, peer_u, peer_v, ln2_g, ln2_b):
    B, S, D = x.shape
    ada = _ada(c, w_ada[0], b_ada[0])
    sh1, sc1, g1, sh2, sc2, g2 = [a.reshape(B, 1, D) for a in jnp.split(ada, 6, axis=-1)]
    phT, qkv0, qkv1, qkv2, gate = _inproj(x, sc1, sh1, positions, w_in[0])
    yhT = _hyena(phT, hy_conv_w[0], hy_conv_b[0], hy_f1_w[0], hy_f1_b[0], hy_f1_freq[0],
                 hy_f2_w[0], hy_f2_b[0], hy_f2_freq[0], hy_f3_w[0], hy_bias[0])
    attn = [_attn_group(v, g) for g, v in enumerate((qkv0, qkv1, qkv2))]
    x1, u2 = _merge(x, yhT, attn, gate, w_branch_hyena[0], w_branch_attn[0], w_out[0],
                    g1, sc2, sh2, ln1_g[0], ln1_b[0])
    out = _peer(u2.reshape(B * S, D), x1.reshape(B * S, D), g2, peer_wq[0], peer_keys[0],
                peer_u[0], peer_v[0], ln2_g[0], ln2_b[0], S)
    return out.reshape(B, S, D)
```
